```python
import math
import jax, jax.numpy as jnp
from jax import lax
import numpy as np

D_MODEL = 2048
BATCH = 2
SEQ = 16384
DEPTH = 2

MIX_WIDTH = D_MODEL
HEAD_DIM = 128
Q_BLOCK = 128
SB_WIDTH = MIX_WIDTH // 2
SB_HEADS = SB_WIDTH // HEAD_DIM
POOL_WINDOWS = (2, 4, 8, 16)
POOL_GROUPS = len(POOL_WINDOWS)
POOL_DIM = (MIX_WIDTH // 2) // POOL_GROUPS
DSA_WIDTH = MIX_WIDTH // 2
DSA_HEADS = DSA_WIDTH // HEAD_DIM
DSA_Q_RANK = D_MODEL // 8
IDX_HEADS = 16
IDX_DIM = 64
DSA_TOPK = 256
DELTA_WIDTH = MIX_WIDTH // 2
DELTA_HEADS = DELTA_WIDTH // HEAD_DIM
DELTA_CONV = 4
DELTA_CHUNK = 64
REL_BUCKETS = 32
REL_MAX_DIST = 2048
MEM_TOKENS = 256
XATTN_HEADS = 4
XATTN_DIM = 128
PEER_HEADS = 8
PEER_KEYS = 128
PEER_EXPERTS = PEER_KEYS * PEER_KEYS
PEER_QUERY_DIM = 256
PEER_TOPK = 16
PEER_BLOCK = 128

kernel_name = 'hybrid_sb_pool_dsa_gdn_peer_trunk'

F32 = jnp.float32


def rms_norm(x, g, eps=1e-6):
    xf = x.astype(F32)
    return (xf * lax.rsqrt(jnp.mean(xf * xf, axis=-1, keepdims=True) + eps)).astype(x.dtype) * g


def l2_norm(t, eps=1e-6):
    tf = t.astype(F32)
    return (tf * lax.rsqrt(jnp.sum(tf * tf, axis=-1, keepdims=True) + eps)).astype(t.dtype)


def split_cols(t, sizes):
    return jnp.split(t, np.cumsum(sizes)[:-1].tolist(), axis=-1)


def to_blocks(t, size):
    b, s = t.shape[:2]
    return jnp.moveaxis(t.reshape(b, s // size, size, *t.shape[2:]), 1, 0)


def from_blocks(t):
    t = jnp.moveaxis(t, 0, 1)
    return t.reshape(t.shape[0], t.shape[1] * t.shape[2], *t.shape[3:])


def t5_bucket(dist):
    n = jnp.maximum(dist, 0)
    exact = REL_BUCKETS // 2
    nf = jnp.maximum(n, 1).astype(F32)
    log_ratio = jnp.log(nf / exact) / math.log(REL_MAX_DIST / exact)
    large = exact + (log_ratio * (REL_BUCKETS - exact)).astype(jnp.int32)
    return jnp.where(n < exact, n, jnp.minimum(large, REL_BUCKETS - 1))


def stick_breaking_attention(q, k, v):
    s = q.shape[1]
    scale = HEAD_DIM ** -0.5
    kpos = jnp.arange(s)

    def block(args):
        q_blk, bi = args
        qpos = bi * Q_BLOCK + jnp.arange(Q_BLOCK)
        z = jnp.einsum('bqhd,bshd->bhqs', q_blk, k).astype(F32) * scale
        before = kpos[None, :] < qpos[:, None]
        log_keep = jnp.where(before, jax.nn.log_sigmoid(-z), 0.0)
        between = lax.cumsum(log_keep, axis=3, reverse=True) - log_keep
        w = jnp.where(before, jnp.exp(jax.nn.log_sigmoid(z) + between), 0.0)
        return jnp.einsum('bhqs,bshd->bqhd', w.astype(v.dtype), v)

    out = lax.map(block, (to_blocks(q, Q_BLOCK), jnp.arange(s // Q_BLOCK)))
    return from_blocks(out)


def multiscale_pool(u, pool_w, pool_scale):
    b, s, c = u.shape
    uf = u.astype(F32)
    cs = jnp.concatenate([jnp.zeros((b, 1, c), F32), jnp.cumsum(uf, axis=1)], axis=1)
    end = jnp.arange(1, s + 1)
    outs = []
    for gi, win in enumerate(POOL_WINDOWS):
        sl = slice(gi * POOL_DIM, (gi + 1) * POOL_DIM)
        start = jnp.maximum(end - win, 0)
        cg = cs[..., sl]
        mean = (cg[:, end] - cg[:, start]) / (end - start).astype(F32)[None, :, None]
        outs.append(mean - uf[..., sl])
    d = jnp.stack(outs, axis=2).astype(u.dtype)
    y = jnp.einsum('bsgc,gcd->bsgd', d, pool_w)
    return y.reshape(b, s, c) * pool_scale


def dsa_sparse_attention(q, k, v, q_idx, k_idx, w_idx, rel_bias):
    s = q.shape[1]
    topk = min(DSA_TOPK, s // 4)
    kpos = jnp.arange(s)
    scale = HEAD_DIM ** -0.5

    def block(args):
        q_blk, qi_blk, wi_blk, bi = args
        qpos = bi * Q_BLOCK + jnp.arange(Q_BLOCK)
        rel = jax.nn.relu(jnp.einsum('bqhd,bsd->bqhs', qi_blk, k_idx) * IDX_DIM ** -0.5)
        score = jnp.einsum('bqh,bqhs->bqs', wi_blk, rel).astype(F32)
        score = jnp.where((kpos[None, :] <= qpos[:, None])[None], score, -jnp.inf)
        _, sel = lax.top_k(score, topk)
        valid = sel <= qpos[None, :, None]
        k_sel = jax.vmap(lambda kb, ib: kb[ib])(k, sel)
        v_sel = jax.vmap(lambda vb, ib: vb[ib])(v, sel)
        bias = rel_bias[t5_bucket(qpos[None, :, None] - sel)]
        logits = jnp.einsum('bqhd,bqkhd->bqhk', q_blk, k_sel).astype(F32) * scale
        logits = logits + jnp.swapaxes(bias, -1, -2).astype(F32)
        logits = jnp.where(valid[:, :, None, :], logits, -jnp.inf)
        p = jax.nn.softmax(logits, axis=-1)
        return jnp.einsum('bqhk,bqkhd->bqhd', p.astype(v.dtype), v_sel)

    xs = (to_blocks(q, Q_BLOCK), to_blocks(q_idx, Q_BLOCK), to_blocks(w_idx, Q_BLOCK), jnp.arange(s // Q_BLOCK))
    return from_blocks(lax.map(block, xs))


def causal_depthwise_conv(x, w):
    kw, c = w.shape
    return lax.conv_general_dilated(x, w[:, None, :].astype(x.dtype), window_strides=(1,),
                                    padding=((kw - 1, 0),), dimension_numbers=('NWC', 'WIO', 'NWC'),
                                    feature_group_count=c)


def gated_delta_rule(q, k, v, g, beta):
    out_dtype = v.dtype
    q, k, v, g, beta = (t.astype(F32) for t in (q, k, v, g, beta))
    b, s, h, dk = q.shape
    dv = v.shape[-1]
    c = DELTA_CHUNK

    def chunks(t):
        return jnp.moveaxis(t.reshape(b, s // c, c, h, *t.shape[3:]), 3, 1)

    qc = chunks(q * dk ** -0.5)
    kc = chunks(k)
    vc = chunks(v)
    gc = jnp.cumsum(chunks(g), axis=-1)
    bc = chunks(beta)
    kb = kc * bc[..., None]
    lower = jnp.tril(jnp.ones((c, c), bool))
    decay = jnp.exp(jnp.where(lower, gc[..., :, None] - gc[..., None, :], -jnp.inf))
    strict = jnp.where(jnp.tril(jnp.ones((c, c), bool), -1),
                       jnp.einsum('bhncd,bhnsd->bhncs', kb, kc) * decay, 0.0)
    unit_lower = strict + jnp.eye(c, dtype=F32)
    rhs = jnp.concatenate([vc * bc[..., None], kb * jnp.exp(gc)[..., None]], axis=-1)
    sol = lax.linalg.triangular_solve(unit_lower, rhs, left_side=True, lower=True, unit_diagonal=True)
    u, w = sol[..., :dv], sol[..., dv:]
    a_intra = jnp.einsum('bhncd,bhnsd->bhncs', qc, kc) * decay

    def step(state, xs):
        q_i, k_i, u_i, w_i, g_i, a_i = xs
        v_new = u_i - jnp.einsum('bhcd,bhde->bhce', w_i, state)
        o = (jnp.einsum('bhcd,bhde->bhce', q_i * jnp.exp(g_i)[..., None], state)
             + jnp.einsum('bhcs,bhse->bhce', a_i, v_new))
        g_last = g_i[..., -1:]
        state = (state * jnp.exp(g_last)[..., None]
                 + jnp.einsum('bhcd,bhce->bhde', k_i * jnp.exp(g_last - g_i)[..., None], v_new))
        return state, o

    xs = tuple(jnp.moveaxis(t, 2, 0) for t in (qc, kc, u, w, gc, a_intra))
    _, o = lax.scan(step, jnp.zeros((b, h, dk, dv), F32), xs)
    return jnp.transpose(o, (1, 0, 3, 2, 4)).reshape(b, s, h, dv).astype(out_dtype)


def stick_pool_mixer(hn, w_in, pool_w, pool_scale, w_out):
    b, s, _ = hn.shape
    q, k, v, u = split_cols(hn @ w_in, [SB_WIDTH, SB_WIDTH, SB_WIDTH, POOL_GROUPS * POOL_DIM])
    heads = lambda t: t.reshape(b, s, SB_HEADS, HEAD_DIM)
    o_a = stick_breaking_attention(heads(q), heads(k), heads(v)).reshape(b, s, SB_WIDTH)
    o_b = multiscale_pool(u, pool_w, pool_scale)
    return jnp.concatenate([o_a, o_b], axis=-1) @ w_out


def dsa_delta_mixer(hn, w_in, w_uq, w_iq, norm_cq, norm_kidx, conv_w, a_log, dt_bias, norm_out, w_out, rel_bias):
    b, s, _ = hn.shape
    c_q, k_c, v_c, k_i, w_i, qkv_d, beta_d, a_d, z_d = split_cols(
        hn @ w_in, [DSA_Q_RANK, DSA_WIDTH, DSA_WIDTH, IDX_DIM, IDX_HEADS,
                    3 * DELTA_WIDTH, DELTA_HEADS, DELTA_HEADS, DELTA_WIDTH])
    c_q = rms_norm(c_q, norm_cq)
    q_c = (c_q @ w_uq).reshape(b, s, DSA_HEADS, HEAD_DIM)
    q_i = (c_q @ w_iq).reshape(b, s, IDX_HEADS, IDX_DIM)
    k_i = rms_norm(k_i, norm_kidx)
    w_i = w_i * IDX_HEADS ** -0.5
    o_c = dsa_sparse_attention(q_c, k_c.reshape(b, s, DSA_HEADS, HEAD_DIM), v_c.reshape(b, s, DSA_HEADS, HEAD_DIM),
                               q_i, k_i, w_i, rel_bias).reshape(b, s, DSA_WIDTH)
    qkv = jax.nn.silu(causal_depthwise_conv(qkv_d, conv_w))
    q_d, k_d, v_d = [t.reshape(b, s, DELTA_HEADS, HEAD_DIM) for t in jnp.split(qkv, 3, axis=-1)]
    beta = jax.nn.sigmoid(beta_d)
    g = -jnp.exp(a_log) * jax.nn.softplus(a_d + dt_bias)
    o_d = gated_delta_rule(l2_norm(q_d), l2_norm(k_d), v_d, g, beta)
    o_d = (rms_norm(o_d, norm_out) * jax.nn.silu(z_d.reshape(b, s, DELTA_HEADS, HEAD_DIM))).reshape(b, s, DELTA_WIDTH)
    return jnp.concatenate([o_c, o_d], axis=-1) @ w_out


def memory_cross_attention(hn, mem_n, w_q, w_kv, w_o):
    b, s, _ = hn.shape
    q = (hn @ w_q).reshape(b, s, XATTN_HEADS, XATTN_DIM)
    kv = (mem_n @ w_kv).reshape(b, mem_n.shape[1], 2, XATTN_HEADS, XATTN_DIM)
    logits = jnp.einsum('bshd,bmhd->bhsm', q, kv[:, :, 0]).astype(F32) * XATTN_DIM ** -0.5
    p = jax.nn.softmax(logits, axis=-1).astype(hn.dtype)
    o = jnp.einsum('bhsm,bmhd->bshd', p, kv[:, :, 1]).reshape(b, s, XATTN_HEADS * XATTN_DIM)
    return o @ w_o


def peer_ffn(hn, w_q, sub_keys, u_tab, v_tab):
    b, s, d = hn.shape
    half = PEER_QUERY_DIM // 2

    def block(xb):
        t = xb.shape[0]
        q = (xb @ w_q).reshape(t, PEER_HEADS, 2, half)
        sub = jnp.einsum('thpd,hpnd->thpn', q, sub_keys).astype(F32)
        sv, si = lax.top_k(sub, PEER_TOPK)
        cand_s = (sv[:, :, 0, :, None] + sv[:, :, 1, None, :]).reshape(t, PEER_HEADS, PEER_TOPK * PEER_TOPK)
        cand_e = (si[:, :, 0, :, None] * PEER_KEYS + si[:, :, 1, None, :]).reshape(t, PEER_HEADS, PEER_TOPK * PEER_TOPK)
        top_s, pos = lax.top_k(cand_s, PEER_TOPK)
        expert = jnp.take_along_axis(cand_e, pos, axis=-1)
        gate = jax.nn.softmax(top_s, axis=-1).astype(xb.dtype)
        act = jax.nn.gelu(jnp.einsum('thkd,td->thk', u_tab[expert], xb), approximate=False)
        return jnp.einsum('thk,thkd->td', gate * act, v_tab[expert])

    out = lax.map(block, hn.reshape(-1, PEER_BLOCK, d))
    return out.reshape(b, s, d)


def setup_inputs(seed: int = 0) -> dict:
    key = jax.random.key(seed)
    ks = jax.random.split(key, 32)
    n_even = (DEPTH + 1) // 2
    n_odd = DEPTH // 2
    nrm = lambda k, shape, scale: jax.random.normal(k, shape, F32) * scale
    gain = lambda k, shape: 1.0 + 0.02 * jax.random.normal(k, shape, F32)
    in_ab = 3 * SB_WIDTH + POOL_GROUPS * POOL_DIM
    in_cd = DSA_Q_RANK + 2 * DSA_WIDTH + IDX_DIM + IDX_HEADS + 4 * DELTA_WIDTH + 2 * DELTA_HEADS
    dt = jnp.exp(jax.random.uniform(ks[17], (n_odd, DELTA_HEADS), F32, math.log(1e-3), math.log(1e-1)))
    return {
        'x': nrm(ks[0], (BATCH, SEQ, D_MODEL), 1.0),
        'mem': nrm(ks[1], (BATCH, MEM_TOKENS, D_MODEL), 1.0),
        'norm_mix': gain(ks[2], (DEPTH, D_MODEL)),
        'norm_cross': gain(ks[3], (DEPTH, D_MODEL)),
        'norm_mem': gain(ks[4], (DEPTH, D_MODEL)),
        'norm_ffn': gain(ks[5], (DEPTH, D_MODEL)),
        'norm_final': gain(ks[6], (D_MODEL,)),
        'w_in_ab': nrm(ks[7], (n_even, D_MODEL, in_ab), D_MODEL ** -0.5),
        'pool_w': nrm(ks[8], (n_even, POOL_GROUPS, POOL_DIM, POOL_DIM), POOL_DIM ** -0.5),
        'pool_scale': gain(ks[9], (n_even, POOL_GROUPS * POOL_DIM)),
        'w_out_ab': nrm(ks[10], (n_even, MIX_WIDTH, D_MODEL), MIX_WIDTH ** -0.5),
        'w_in_cd': nrm(ks[11], (n_odd, D_MODEL, in_cd), D_MODEL ** -0.5),
        'w_uq': nrm(ks[12], (n_odd, DSA_Q_RANK, DSA_WIDTH), DSA_Q_RANK ** -0.5),
        'w_iq': nrm(ks[13], (n_odd, DSA_Q_RANK, IDX_HEADS * IDX_DIM), DSA_Q_RANK ** -0.5),
        'norm_cq': gain(ks[14], (n_odd, DSA_Q_RANK)),
        'norm_kidx': gain(ks[15], (n_odd, IDX_DIM)),
        'conv_w': nrm(ks[16], (n_odd, DELTA_CONV, 3 * DELTA_WIDTH), DELTA_CONV ** -0.5),
        'a_log': jnp.log(jax.random.uniform(ks[18], (n_odd, DELTA_HEADS), F32, 1.0, 16.0)),
        'dt_bias': dt + jnp.log(-jnp.expm1(-dt)),
        'norm_delta_out': gain(ks[19], (n_odd, HEAD_DIM)),
        'w_out_cd': nrm(ks[20], (n_odd, MIX_WIDTH, D_MODEL), MIX_WIDTH ** -0.5),
        'rel_bias': nrm(ks[21], (REL_BUCKETS, DSA_HEADS), 0.5),
        'xattn_wq': nrm(ks[22], (DEPTH, D_MODEL, XATTN_HEADS * XATTN_DIM), D_MODEL ** -0.5),
        'xattn_wkv': nrm(ks[23], (DEPTH, D_MODEL, 2 * XATTN_HEADS * XATTN_DIM), D_MODEL ** -0.5),
        'xattn_wo': nrm(ks[24], (DEPTH, XATTN_HEADS * XATTN_DIM, D_MODEL), (XATTN_HEADS * XATTN_DIM) ** -0.5),
        'peer_wq': nrm(ks[25], (DEPTH, D_MODEL, PEER_HEADS * PEER_QUERY_DIM), D_MODEL ** -0.5),
        'peer_subkeys': nrm(ks[26], (DEPTH, PEER_HEADS, 2, PEER_KEYS, PEER_QUERY_DIM // 2), (PEER_QUERY_DIM // 2) ** -0.5),
        'peer_u': nrm(ks[27], (DEPTH, PEER_EXPERTS, D_MODEL), D_MODEL ** -0.5),
        'peer_v': nrm(ks[28], (DEPTH, PEER_EXPERTS, D_MODEL), (PEER_HEADS * PEER_TOPK) ** -0.5),
    }


def reference(x, mem, norm_mix, norm_cross, norm_mem, norm_ffn, norm_final, w_in_ab, pool_w, pool_scale,
              w_out_ab, w_in_cd, w_uq, w_iq, norm_cq, norm_kidx, conv_w, a_log, dt_bias, norm_delta_out,
              w_out_cd, rel_bias, xattn_wq, xattn_wkv, xattn_wo, peer_wq, peer_subkeys, peer_u, peer_v):
    h = x
    for layer in range(DEPTH):
        j = layer // 2
        hn = rms_norm(h, norm_mix[layer])
        if layer % 2 == 0:
            mix = stick_pool_mixer(hn, w_in_ab[j], pool_w[j], pool_scale[j], w_out_ab[j])
        else:
            mix = dsa_delta_mixer(hn, w_in_cd[j], w_uq[j], w_iq[j], norm_cq[j], norm_kidx[j], conv_w[j],
                                  a_log[j], dt_bias[j], norm_delta_out[j], w_out_cd[j], rel_bias)
        h = h + mix
        h = h + memory_cross_attention(rms_norm(h, norm_cross[layer]), rms_norm(mem, norm_mem[layer]),
                                       xattn_wq[layer], xattn_wkv[layer], xattn_wo[layer])
        h = h + peer_ffn(rms_norm(h, norm_ffn[layer]), peer_wq[layer], peer_subkeys[layer],
                         peer_u[layer], peer_v[layer])
    return rms_norm(h, norm_final)
```

```python
import functools
import math

import jax
import jax.numpy as jnp
from jax import lax
from jax.experimental import pallas as pl
from jax.experimental.pallas import tpu as pltpu

F32 = jnp.float32
BF16 = jnp.bfloat16

HEAD_DIM = 128
RMS_EPS = 1e-6
EXP_UNDERFLOW = -104.0
VMEM_LIMIT_BYTES = 56 * 1024 * 1024


def _cparams(*semantics):
    return pltpu.CompilerParams(dimension_semantics=semantics, vmem_limit_bytes=VMEM_LIMIT_BYTES)


def _nt_dot(a, b):
    return lax.dot_general(a, b, (((1,), (1,)), ((), ())), preferred_element_type=F32)


def _dot(a, b):
    return jnp.dot(a, b, preferred_element_type=F32)


def _rms_norm_rows(xf, g):
    return xf * lax.rsqrt(jnp.mean(xf * xf, axis=-1, keepdims=True) + RMS_EPS) * g


def _linear_kernel(*refs, n_lhs, has_norm, has_res):
    pos = 0
    x_refs = refs[pos:pos + n_lhs]; pos += n_lhs
    w_refs = refs[pos:pos + n_lhs]; pos += n_lhs
    g_ref = None
    if has_norm:
        g_ref = refs[pos]; pos += 1
    r_ref = None
    if has_res:
        r_ref = refs[pos]; pos += 1
    o_ref = refs[pos]; pos += 1
    xn_ref = refs[pos] if has_norm else None

    if has_norm:
        @pl.when(pl.program_id(1) == 0)
        def _():
            xn_ref[...] = _rms_norm_rows(x_refs[0][...].astype(F32), g_ref[...]).astype(BF16)
        acc = _dot(xn_ref[...], w_refs[0][...])
    else:
        acc = _dot(x_refs[0][...].astype(BF16), w_refs[0][...])
        for x_ref, w_ref in zip(x_refs[1:], w_refs[1:]):
            acc = acc + _dot(x_ref[...].astype(BF16), w_ref[...])
    if has_res:
        acc = acc + r_ref[...]
    o_ref[...] = acc.astype(o_ref.dtype)


def _linear(xs, ws, *, norm_g=None, residual=None, out_dtype=F32, tm=512, tn=None, name="linear"):
    xs = list(xs)
    ws = [w.astype(BF16) for w in ws]
    t = xs[0].shape[0]
    n = ws[0].shape[1]
    if tn is None:
        tn = n
    assert t % tm == 0 and n % tn == 0
    has_norm = norm_g is not None
    has_res = residual is not None
    assert not has_norm or len(xs) == 1
    in_specs = [pl.BlockSpec((tm, x.shape[1]), lambda i, j: (i, 0)) for x in xs]
    in_specs += [pl.BlockSpec((w.shape[0], tn), lambda i, j: (0, j)) for w in ws]
    args = xs + ws
    if has_norm:
        in_specs.append(pl.BlockSpec((1, xs[0].shape[1]), lambda i, j: (0, 0)))
        args.append(norm_g.reshape(1, -1).astype(F32))
    if has_res:
        in_specs.append(pl.BlockSpec((tm, tn), lambda i, j: (i, j)))
        args.append(residual)
    scratch = [pltpu.VMEM((tm, xs[0].shape[1]), BF16)] if has_norm else []
    return pl.pallas_call(
        functools.partial(_linear_kernel, n_lhs=len(xs), has_norm=has_norm, has_res=has_res),
        grid=(t // tm, n // tn),
        in_specs=in_specs,
        out_specs=pl.BlockSpec((tm, tn), lambda i, j: (i, j)),
        out_shape=jax.ShapeDtypeStruct((t, n), out_dtype),
        scratch_shapes=scratch,
        compiler_params=_cparams("parallel", "arbitrary"),
        name=name,
    )(*args)


SB_BLOCK = 256


def _sb_attn_kernel(q_ref, k_ref, v_ref, o_ref):
    tq = q_ref.shape[0]
    qi = pl.program_id(2)
    scale = HEAD_DIM ** -0.5
    q = q_ref[...]
    row = lax.broadcasted_iota(jnp.int32, (tq, tq), 0)
    col = lax.broadcasted_iota(jnp.int32, (tq, tq), 1)
    suffix = jnp.where(row > col, 1.0, 0.0).astype(BF16)
    before = col < row

    def block(j, carry, acc, diagonal):
        start = pl.multiple_of(j * tq, tq)
        k = k_ref[pl.ds(start, tq), :]
        v = v_ref[pl.ds(start, tq), :]
        z = _nt_dot(q, k) * scale
        lk = -(jnp.maximum(z, 0.0) + jnp.log1p(jnp.exp(-jnp.abs(z))))
        if diagonal:
            lk = jnp.where(before, lk, 0.0)
        lk_hi = lk.astype(BF16)
        lk_lo = (lk - lk_hi.astype(F32)).astype(BF16)
        right = _dot(lk_hi, suffix) + _dot(lk_lo, suffix)
        w = jnp.exp(z + lk + right + carry)
        if diagonal:
            w = jnp.where(before, w, 0.0)
        acc = acc + _dot(w.astype(BF16), v)
        carry = carry + right[:, 0:1] + lk[:, 0:1]
        return carry, acc

    carry, acc = block(qi, jnp.zeros((tq, 1), F32), jnp.zeros((tq, HEAD_DIM), F32), True)

    def cond(state):
        j, carry, _ = state
        return jnp.logical_and(j >= 0, jnp.max(carry) > EXP_UNDERFLOW)

    def body(state):
        j, carry, acc = state
        carry, acc = block(j, carry, acc, False)
        return j - 1, carry, acc

    _, _, acc = lax.while_loop(cond, body, (qi - 1, carry, acc))
    o_ref[...] = acc.astype(o_ref.dtype)


def _sb_attention(qkv, batch, seq, heads):
    tq = SB_BLOCK
    nq = seq // tq
    return pl.pallas_call(
        _sb_attn_kernel,
        grid=(batch, heads, nq),
        in_specs=[
            pl.BlockSpec((tq, HEAD_DIM), lambda b, h, i: (b * nq + i, h)),
            pl.BlockSpec((seq, HEAD_DIM), lambda b, h, i: (b, heads + h)),
            pl.BlockSpec((seq, HEAD_DIM), lambda b, h, i: (b, 2 * heads + h)),
        ],
        out_specs=pl.BlockSpec((tq, HEAD_DIM), lambda b, h, i: (b * nq + i, h)),
        out_shape=jax.ShapeDtypeStruct((batch * seq, heads * HEAD_DIM), BF16),
        compiler_params=_cparams("parallel", "parallel", "arbitrary"),
        name="sb_attention",
    )(qkv, qkv, qkv)


POOL_WINDOWS = (2, 4, 8, 16)
POOL_DIM = 256
POOL_HALO = 16


def _pool_kernel(u_ref, halo_ref, w_ref, scale_ref, o_ref, *, tiles_per_seq):
    tm = u_ref.shape[0]
    tile_in_seq = pl.program_id(0) % tiles_per_seq
    u = u_ref[...]
    halo = jnp.where(tile_in_seq == 0, 0.0, halo_ref[...])
    ext = jnp.concatenate([halo, u], axis=0)
    pos = tile_in_seq * tm + lax.broadcasted_iota(jnp.int32, (tm, 1), 0)
    for g, win in enumerate(POOL_WINDOWS):
        cols = slice(g * POOL_DIM, (g + 1) * POOL_DIM)
        a = ext[:, cols]
        k = 1
        while k < win:
            a = a + pltpu.roll(a, k, 0)
            k *= 2
        count = jnp.minimum(pos + 1, win).astype(F32)
        d = a[POOL_HALO:, :] / count - u[:, cols]
        y = _dot(d.astype(BF16), w_ref[g]) * scale_ref[:, cols]
        o_ref[:, cols] = y.astype(o_ref.dtype)


def _multiscale_pool(u, pool_w, pool_scale, seq, tm=512):
    t, c = u.shape
    assert seq % tm == 0 and tm % POOL_HALO == 0 and max(POOL_WINDOWS) <= POOL_HALO
    halo_blocks_per_tile = tm // POOL_HALO
    return pl.pallas_call(
        functools.partial(_pool_kernel, tiles_per_seq=seq // tm),
        grid=(t // tm,),
        in_specs=[
            pl.BlockSpec((tm, c), lambda i: (i, 0)),
            pl.BlockSpec((POOL_HALO, c), lambda i: (jnp.maximum(i * halo_blocks_per_tile - 1, 0), 0)),
            pl.BlockSpec(pool_w.shape, lambda i: (0, 0, 0)),
            pl.BlockSpec((1, c), lambda i: (0, 0)),
        ],
        out_specs=pl.BlockSpec((tm, c), lambda i: (i, 0)),
        out_shape=jax.ShapeDtypeStruct((t, c), BF16),
        compiler_params=_cparams("parallel"),
        name="multiscale_pool",
    )(u, u, pool_w.astype(BF16), pool_scale.reshape(1, c).astype(F32))


XATTN_HEADS = 4


def _xattn_kernel(h_ref, g_ref, wq_ref, k_ref, v_ref, wo_ref, o_ref):
    scale = HEAD_DIM ** -0.5
    h = h_ref[...]
    hn = _rms_norm_rows(h, g_ref[...]).astype(BF16)
    q = _dot(hn, wq_ref[...]).astype(BF16)
    outs = []
    for hd in range(XATTN_HEADS):
        cols = slice(hd * HEAD_DIM, (hd + 1) * HEAD_DIM)
        logits = _nt_dot(q[:, cols], k_ref[:, cols]) * scale
        logits = logits - jnp.max(logits, axis=-1, keepdims=True)
        e = jnp.exp(logits)
        p = e / jnp.sum(e, axis=-1, keepdims=True)
        outs.append(_dot(p.astype(BF16), v_ref[:, cols]))
    o = jnp.concatenate(outs, axis=-1).astype(BF16)
    o_ref[...] = h + _dot(o, wo_ref[...])


def _cross_attention(h, norm_g, wq, kv, wo, seq, mem_tokens, tm=512):
    t, d = h.shape
    hw = XATTN_HEADS * HEAD_DIM
    tiles_per_seq = seq // tm
    return pl.pallas_call(
        _xattn_kernel,
        grid=(t // tm,),
        in_specs=[
            pl.BlockSpec((tm, d), lambda i: (i, 0)),
            pl.BlockSpec((1, d), lambda i: (0, 0)),
            pl.BlockSpec((d, hw), lambda i: (0, 0)),
            pl.BlockSpec((mem_tokens, hw), lambda i: (i // tiles_per_seq, 0)),
            pl.BlockSpec((mem_tokens, hw), lambda i: (i // tiles_per_seq, 1)),
            pl.BlockSpec((hw, d), lambda i: (0, 0)),
        ],
        out_specs=pl.BlockSpec((tm, d), lambda i: (i, 0)),
        out_shape=jax.ShapeDtypeStruct((t, d), F32),
        compiler_params=_cparams("parallel"),
        name="cross_attention",
    )(h, norm_g.reshape(1, d).astype(F32), wq.astype(BF16), kv, kv, wo.astype(BF16))


def _rmsnorm_kernel(x_ref, g_ref, o_ref):
    o_ref[...] = _rms_norm_rows(x_ref[...], g_ref[...])


def _rmsnorm(x, g, tm=512):
    t, d = x.shape
    return pl.pallas_call(
        _rmsnorm_kernel,
        grid=(t // tm,),
        in_specs=[pl.BlockSpec((tm, d), lambda i: (i, 0)), pl.BlockSpec((1, d), lambda i: (0, 0))],
        out_specs=pl.BlockSpec((tm, d), lambda i: (i, 0)),
        out_shape=jax.ShapeDtypeStruct((t, d), F32),
        compiler_params=_cparams("parallel"),
        name="final_rmsnorm",
    )(x, g.reshape(1, d).astype(F32))


PEER_HEADS = 8
PEER_KEYS = 128
PEER_TOPK = 16
NEG_INF = float("-inf")


def _top16_rows(s, vals_ref):
    rows = lax.broadcasted_iota(jnp.int32, s.shape, 0)
    work = s
    for r in range(PEER_TOPK):
        m = jnp.max(work, axis=0, keepdims=True)
        first = jnp.min(jnp.where(work == m, rows, PEER_KEYS), axis=0, keepdims=True)
        vals_ref[r:r + 1, :] = m
        work = jnp.where(rows == first, NEG_INF, work)
    return work == NEG_INF


def _peer_route_kernel(q_ref, keys_ref, s1_ref, c1_ref, s2_ref, e2_ref, tau_ref, a_ref, b_ref, cand_ref):
    for hd in range(PEER_HEADS):
        s1 = _nt_dot(keys_ref[2 * hd], q_ref[:, (2 * hd) * PEER_KEYS:(2 * hd + 1) * PEER_KEYS])
        s2 = _nt_dot(keys_ref[2 * hd + 1], q_ref[:, (2 * hd + 1) * PEER_KEYS:(2 * hd + 2) * PEER_KEYS])
        top1 = _top16_rows(s1, a_ref)
        top2 = _top16_rows(s2, b_ref)
        a = a_ref[...]
        b = b_ref[...]
        for i in range(PEER_TOPK):
            cand_ref[i * PEER_TOPK:(i + 1) * PEER_TOPK, :] = a[i:i + 1, :] + b
        work = cand_ref[...]
        best = a[0:1, :] + b[0:1, :]
        taken = jnp.zeros_like(best)
        tau = best
        z = jnp.zeros_like(best)
        for _ in range(PEER_TOPK):
            m = jnp.max(work, axis=0, keepdims=True)
            eq = work == m
            cnt = jnp.sum(jnp.where(eq, 1.0, 0.0), axis=0, keepdims=True)
            room = PEER_TOPK - taken
            use = jnp.clip(jnp.minimum(cnt, room), 0.0, None)
            z = z + use * jnp.exp(m - best)
            tau = jnp.where(room > 0.0, m, tau)
            taken = taken + cnt
            work = jnp.where(eq, NEG_INF, work)
        s1_ref[hd] = jnp.where(top1, s1, NEG_INF)
        c1_ref[hd] = jnp.where(top1, jnp.exp(s1 - a[0:1, :]), 0.0) / z
        s2_ref[hd] = jnp.where(top2, s2, NEG_INF)
        e2_ref[hd] = jnp.exp(s2 - b[0:1, :])
        tau_ref[hd:hd + 1, :] = tau


def _peer_route(q, subkeys, tm=256):
    t = q.shape[0]
    keys = subkeys.reshape(PEER_HEADS * 2, PEER_KEYS, PEER_KEYS).astype(BF16)
    big = jax.ShapeDtypeStruct((PEER_HEADS, PEER_KEYS, t), F32)
    big_spec = pl.BlockSpec((PEER_HEADS, PEER_KEYS, tm), lambda i: (0, 0, i))
    return pl.pallas_call(
        _peer_route_kernel,
        grid=(t // tm,),
        in_specs=[
            pl.BlockSpec((tm, q.shape[1]), lambda i: (i, 0)),
            pl.BlockSpec(keys.shape, lambda i: (0, 0, 0)),
        ],
        out_specs=[big_spec, big_spec, big_spec, big_spec, pl.BlockSpec((PEER_HEADS, tm), lambda i: (0, i))],
        out_shape=[big, big, big, big, jax.ShapeDtypeStruct((PEER_HEADS, t), F32)],
        scratch_shapes=[
            pltpu.VMEM((PEER_TOPK, tm), F32),
            pltpu.VMEM((PEER_TOPK, tm), F32),
            pltpu.VMEM((PEER_TOPK * PEER_TOPK, tm), F32),
        ],
        compiler_params=_cparams("parallel"),
        name="peer_route",
    )(q, keys)


PEER_I1_PER_TILE = 8


def _peer_expert_kernel(h_ref, g_ref, u_ref, v_ref, s1_ref, c1_ref, s2_ref, e2_ref, tau_ref, o_ref,
                        xn_ref, acc_ref):
    e = pl.program_id(1)

    @pl.when(e == 0)
    def _():
        xn_ref[...] = _rms_norm_rows(h_ref[...], g_ref[...]).astype(BF16)
        acc_ref[...] = jnp.zeros_like(acc_ref)

    act = _nt_dot(u_ref[...], xn_ref[...])
    act = 0.5 * act * (1.0 + lax.erf(act * (2.0 ** -0.5)))
    gates = []
    for i1 in range(PEER_I1_PER_TILE):
        gate = None
        for hd in range(PEER_HEADS):
            pair = s1_ref[hd, i1:i1 + 1, :] + s2_ref[hd]
            term = c1_ref[hd, i1:i1 + 1, :] * jnp.where(pair >= tau_ref[hd:hd + 1, :], e2_ref[hd], 0.0)
            gate = term if gate is None else gate + term
        gates.append(gate)
    weighted = (jnp.concatenate(gates, axis=0) * act).astype(BF16)
    acc_ref[...] += lax.dot_general(weighted, v_ref[...], (((0,), (0,)), ((), ())),
                                    preferred_element_type=F32)

    @pl.when(e == pl.num_programs(1) - 1)
    def _():
        o_ref[...] = h_ref[...] + acc_ref[...]


def _peer_experts(h, norm_g, u_tab, v_tab, route, tm=512):
    s1, c1, s2, e2, tau = route
    t, d = h.shape
    n_exp = u_tab.shape[0]
    te = PEER_I1_PER_TILE * PEER_KEYS
    head_rows = pl.BlockSpec((PEER_HEADS, PEER_I1_PER_TILE, tm), lambda i, e: (0, e, i))
    head_full = pl.BlockSpec((PEER_HEADS, PEER_KEYS, tm), lambda i, e: (0, 0, i))
    return pl.pallas_call(
        _peer_expert_kernel,
        grid=(t // tm, n_exp // te),
        in_specs=[
            pl.BlockSpec((tm, d), lambda i, e: (i, 0)),
            pl.BlockSpec((1, d), lambda i, e: (0, 0)),
            pl.BlockSpec((te, d), lambda i, e: (e, 0)),
            pl.BlockSpec((te, d), lambda i, e: (e, 0)),
            head_rows, head_rows, head_full, head_full,
            pl.BlockSpec((PEER_HEADS, tm), lambda i, e: (0, i)),
        ],
        out_specs=pl.BlockSpec((tm, d), lambda i, e: (i, 0)),
        out_shape=jax.ShapeDtypeStruct((t, d), F32),
        scratch_shapes=[pltpu.VMEM((tm, d), BF16), pltpu.VMEM((tm, d), F32)],
        compiler_params=_cparams("parallel", "arbitrary"),
        name="peer_experts",
    )(h, norm_g.reshape(1, d).astype(F32), u_tab.astype(BF16), v_tab.astype(BF16), s1, c1, s2, e2, tau)


def _peer_ffn(h, norm_g, w_q, subkeys, u_tab, v_tab):
    q = _linear([h], [w_q], norm_g=norm_g, out_dtype=BF16, tn=1024, name="peer_q")
    return _peer_experts(h, norm_g, u_tab, v_tab, _peer_route(q, subkeys))


DSA_HEADS = 8
IDX_HEADS = 16
IDX_DIM = 64
DSA_TOPK = 256
DSA_BLOCK = 128
MASKED = -1e30
REL_BUCKETS = 32
REL_MAX_DIST = 2048
DSA_NEAR_BLOCKS = REL_MAX_DIST // DSA_BLOCK + 1
INT32_MIN = -2 ** 31


def _dsa_select_kernel(qi_ref, w_ref, kidx_ref, mask_ref, key_ref, *, topk):
    ck = DSA_BLOCK
    tq = qi_ref.shape[0]
    n_total = mask_ref.shape[0] // ck
    qb = pl.program_id(1)
    n_chunks = qb + 1
    qi = qi_ref[...]
    w = w_ref[...] * (IDX_HEADS ** -0.5)
    qpos = qb * tq + lax.broadcasted_iota(jnp.int32, (1, tq), 1)
    krow = lax.broadcasted_iota(jnp.int32, (ck, 1), 0)

    def score_chunk(c, carry):
        start = pl.multiple_of(c * ck, ck)
        kc = kidx_ref[pl.ds(start, ck), :]
        sc = jnp.zeros((ck, tq), F32)
        for hd in range(IDX_HEADS):
            rel = jnp.maximum(_nt_dot(kc, qi[:, hd * IDX_DIM:(hd + 1) * IDX_DIM]) * (IDX_DIM ** -0.5), 0.0)
            sc = sc + w[hd:hd + 1, :] * rel
        sc = jnp.where(start + krow <= qpos, sc + 0.0, NEG_INF)
        bits = pltpu.bitcast(sc, jnp.int32)
        key_ref[pl.ds(start, ck), :] = jnp.where(bits < 0, bits ^ jnp.int32(0x7FFFFFFF), bits)
        return carry

    lax.fori_loop(0, n_chunks, score_chunk, 0)

    def count_ge(cand):
        def chunk(c, acc):
            start = pl.multiple_of(c * ck, ck)
            hit = jnp.where(key_ref[pl.ds(start, ck), :] >= cand, 1, 0)
            return acc + jnp.sum(hit.reshape(ck // 8, 8, tq), axis=0)
        acc = lax.fori_loop(0, n_chunks, chunk, jnp.zeros((8, tq), jnp.int32))
        return jnp.sum(acc, axis=0, keepdims=True)

    def refine(i, cur):
        cand = cur + lax.shift_left(jnp.int32(1), 31 - i)
        return jnp.where(count_ge(cand) >= topk, cand, cur)

    kth = lax.fori_loop(0, 32, refine, jnp.full((1, tq), INT32_MIN, jnp.int32))

    def write_mask(c, carry):
        start = pl.multiple_of(c * ck, ck)
        keep = jnp.logical_and(key_ref[pl.ds(start, ck), :] >= kth, start + krow <= qpos)
        mask_ref[pl.ds(start, ck), :] = jnp.where(keep, 0.0, MASKED).astype(mask_ref.dtype)
        return carry

    lax.fori_loop(0, n_chunks, write_mask, 0)

    def write_rest(c, carry):
        start = pl.multiple_of(c * ck, ck)
        mask_ref[pl.ds(start, ck), :] = jnp.full((ck, tq), MASKED, mask_ref.dtype)
        return carry

    lax.fori_loop(n_chunks, n_total, write_rest, 0)


def _dsa_select(qq, w_t, kidx, batch, seq):
    tq = DSA_BLOCK
    nq = seq // tq
    topk = min(DSA_TOPK, seq // 4)
    iw = IDX_HEADS * IDX_DIM
    return pl.pallas_call(
        functools.partial(_dsa_select_kernel, topk=topk),
        grid=(batch, nq),
        in_specs=[
            pl.BlockSpec((tq, iw), lambda b, i: (b * nq + i, 1)),
            pl.BlockSpec((IDX_HEADS, tq), lambda b, i: (0, b * nq + i)),
            pl.BlockSpec((seq, IDX_DIM), lambda b, i: (b, 0)),
        ],
        out_specs=pl.BlockSpec((None, seq, tq), lambda b, i: (b, 0, i)),
        out_shape=jax.ShapeDtypeStruct((batch, seq, seq), BF16),
        scratch_shapes=[pltpu.VMEM((seq, tq), jnp.int32)],
        compiler_params=_cparams("parallel", "arbitrary"),
        name="dsa_select",
    )(qq, w_t, kidx)


def _dsa_attn_kernel(q_ref, k_ref, v_ref, mask_ref, bias_ref, far_ref, o_ref):
    ck = DSA_BLOCK
    tq = q_ref.shape[0]
    i = pl.program_id(2)
    scale = HEAD_DIM ** -0.5
    q = q_ref[...]

    def step(c, state, bias):
        m, l, acc = state
        start = pl.multiple_of(c * ck, ck)
        kc = k_ref[pl.ds(start, ck), :]
        vc = v_ref[pl.ds(start, ck), :]
        s = _nt_dot(kc, q) * scale + bias + mask_ref[pl.ds(start, ck), :].astype(F32)
        m_new = jnp.maximum(m, jnp.max(s, axis=0, keepdims=True))
        alpha = jnp.exp(m - m_new)
        p = jnp.exp(s - m_new)
        l = l * alpha + jnp.sum(p, axis=0, keepdims=True)
        pv = lax.dot_general(vc, p.astype(BF16), (((0,), (0,)), ((), ())), preferred_element_type=F32)
        return m_new, l, acc * alpha + pv

    state = (jnp.full((1, tq), MASKED, F32), jnp.zeros((1, tq), F32), jnp.zeros((HEAD_DIM, tq), F32))
    n_far = jnp.maximum(i - (DSA_NEAR_BLOCKS - 1), 0)
    far = far_ref[...]
    state = lax.fori_loop(0, n_far, lambda c, st: step(c, st, far), state)
    state = lax.fori_loop(n_far, i + 1, lambda c, st: step(c, st, bias_ref[i - c]), state)
    _, l, acc = state
    o_ref[...] = (acc / l).T.astype(o_ref.dtype)


def _t5_bucket(dist):
    n = jnp.maximum(dist, 0)
    exact = REL_BUCKETS // 2
    nf = jnp.maximum(n, 1).astype(F32)
    log_ratio = jnp.log(nf / exact) / math.log(REL_MAX_DIST / exact)
    large = exact + (log_ratio * (REL_BUCKETS - exact)).astype(jnp.int32)
    return jnp.where(n < exact, n, jnp.minimum(large, REL_BUCKETS - 1))


def _dsa_bias_tables(rel_bias):
    blk = DSA_BLOCK
    delta = jnp.arange(DSA_NEAR_BLOCKS)[:, None, None] * blk
    dist = delta + jnp.arange(blk)[None, None, :] - jnp.arange(blk)[None, :, None]
    near = jnp.moveaxis(rel_bias[_t5_bucket(dist)], -1, 0)
    far = jnp.broadcast_to(rel_bias[REL_BUCKETS - 1][:, None, None], (rel_bias.shape[1], 1, blk))
    return near.astype(F32), far.astype(F32)


def _dsa_attention(qq, kv, mask, rel_bias, batch, seq):
    tq = DSA_BLOCK
    nq = seq // tq
    near, far = _dsa_bias_tables(rel_bias)
    return pl.pallas_call(
        _dsa_attn_kernel,
        grid=(batch, DSA_HEADS, nq),
        in_specs=[
            pl.BlockSpec((tq, HEAD_DIM), lambda b, h, i: (b * nq + i, h)),
            pl.BlockSpec((seq, HEAD_DIM), lambda b, h, i: (b, h)),
            pl.BlockSpec((seq, HEAD_DIM), lambda b, h, i: (b, DSA_HEADS + h)),
            pl.BlockSpec((None, seq, tq), lambda b, h, i: (b, 0, i)),
            pl.BlockSpec((None, DSA_NEAR_BLOCKS, tq, tq), lambda b, h, i: (h, 0, 0, 0)),
            pl.BlockSpec((None, 1, tq), lambda b, h, i: (h, 0, 0)),
        ],
        out_specs=pl.BlockSpec((tq, HEAD_DIM), lambda b, h, i: (b * nq + i, h)),
        out_shape=jax.ShapeDtypeStruct((batch * seq, DSA_HEADS * HEAD_DIM), BF16),
        compiler_params=_cparams("parallel", "parallel", "arbitrary"),
        name="dsa_attention",
    )(qq, kv, kv, mask, near, far)


DELTA_HEADS = 8
DELTA_CONV = 4
DELTA_CHUNK = 64
DELTA_WIDTH = DELTA_HEADS * HEAD_DIM
CONV_HALO = 8
L2_EPS = 1e-6
GDN_CHUNKS_PER_STEP = 8
GDN_PRECISION = lax.Precision.HIGHEST


def _softplus(x):
    return jnp.maximum(x, 0.0) + jnp.log1p(jnp.exp(-jnp.abs(x)))


def _sigmoid(x):
    return 1.0 / (1.0 + jnp.exp(-x))


def _cd_small_kernel(x_ref, gk_ref, alog_ref, dt_ref, kidx_ref, w_ref, beta_ref, g_ref):
    x = x_ref[...]
    kidx_ref[...] = _rms_norm_rows(x[:, :IDX_DIM], gk_ref[...]).astype(kidx_ref.dtype)
    o = IDX_DIM
    w_ref[...] = x[:, o:o + IDX_HEADS]
    o += IDX_HEADS
    beta_ref[...] = _sigmoid(x[:, o:o + DELTA_HEADS])
    o += DELTA_HEADS
    g_ref[...] = -jnp.exp(alog_ref[...]) * _softplus(x[:, o:o + DELTA_HEADS] + dt_ref[...])


def _cd_small(small, norm_kidx, a_log, dt_bias, tm=1024):
    t, c = small.shape
    row = lambda n: pl.BlockSpec((1, n), lambda i: (0, 0))
    out = lambda n: pl.BlockSpec((tm, n), lambda i: (i, 0))
    return pl.pallas_call(
        _cd_small_kernel,
        grid=(t // tm,),
        in_specs=[pl.BlockSpec((tm, c), lambda i: (i, 0)), row(IDX_DIM), row(DELTA_HEADS), row(DELTA_HEADS)],
        out_specs=[out(IDX_DIM), out(IDX_HEADS), out(DELTA_HEADS), out(DELTA_HEADS)],
        out_shape=[jax.ShapeDtypeStruct((t, IDX_DIM), BF16), jax.ShapeDtypeStruct((t, IDX_HEADS), F32),
                   jax.ShapeDtypeStruct((t, DELTA_HEADS), F32), jax.ShapeDtypeStruct((t, DELTA_HEADS), F32)],
        compiler_params=_cparams("parallel"),
        name="cd_small_prep",
    )(small, norm_kidx.reshape(1, -1).astype(F32), a_log.reshape(1, -1).astype(F32),
      dt_bias.reshape(1, -1).astype(F32))


def _gdn_conv_kernel(x_ref, halo_ref, w_ref, o_ref, *, tiles_per_seq):
    tm = x_ref.shape[0]
    first = pl.program_id(0) % tiles_per_seq == 0
    ext = jnp.concatenate([jnp.where(first, 0.0, halo_ref[...]), x_ref[...]], axis=0)
    w = w_ref[...]
    y = ext * w[DELTA_CONV - 1:DELTA_CONV, :]
    for back in range(1, DELTA_CONV):
        y = y + pltpu.roll(ext, back, 0) * w[DELTA_CONV - 1 - back:DELTA_CONV - back, :]
    y = y[CONV_HALO:, :]
    y = y * _sigmoid(y)
    for hd in range(3 * DELTA_HEADS):
        cols = slice(hd * HEAD_DIM, (hd + 1) * HEAD_DIM)
        t = y[:, cols]
        if hd < 2 * DELTA_HEADS:
            t = t * lax.rsqrt(jnp.sum(t * t, axis=-1, keepdims=True) + L2_EPS)
        if hd < DELTA_HEADS:
            t = t * (HEAD_DIM ** -0.5)
        o_ref[:, cols] = t


def _gdn_conv(x, conv_w, seq, tm=256):
    t = x.shape[0]
    c = conv_w.shape[1]
    halo_blocks_per_tile = tm // CONV_HALO
    return pl.pallas_call(
        functools.partial(_gdn_conv_kernel, tiles_per_seq=seq // tm),
        grid=(t // tm,),
        in_specs=[
            pl.BlockSpec((tm, c), lambda i: (i, 0)),
            pl.BlockSpec((CONV_HALO, c), lambda i: (jnp.maximum(i * halo_blocks_per_tile - 1, 0), 0)),
            pl.BlockSpec((DELTA_CONV, c), lambda i: (0, 0)),
        ],
        out_specs=pl.BlockSpec((tm, c), lambda i: (i, 0)),
        out_shape=jax.ShapeDtypeStruct((t, c), F32),
        compiler_params=_cparams("parallel"),
        name="gdn_conv",
    )(x, x, conv_w.astype(F32))


def _pdot(a, b):
    return jnp.dot(a, b, preferred_element_type=F32, precision=GDN_PRECISION)


def _pdot_nt(a, b):
    return lax.dot_general(a, b, (((1,), (1,)), ((), ())), preferred_element_type=F32, precision=GDN_PRECISION)


def _pdot_tn(a, b):
    return lax.dot_general(a, b, (((0,), (0,)), ((), ())), preferred_element_type=F32, precision=GDN_PRECISION)


def _unit_lower_inverse(a):
    n = a.shape[0]
    eye = jnp.where(lax.broadcasted_iota(jnp.int32, (n, n), 0) == lax.broadcasted_iota(jnp.int32, (n, n), 1), 1.0, 0.0)
    inv = eye - a
    power = a
    span = 2
    while span < n:
        power = _pdot(power, power)
        inv = inv + _pdot(inv, power)
        span *= 2
    return inv


def _gdn_kernel(q_ref, k_ref, v_ref, z_ref, g_ref, beta_ref, gain_ref, o_ref, state_ref):
    c = DELTA_CHUNK

    @pl.when(pl.program_id(2) == 0)
    def _():
        state_ref[...] = jnp.zeros_like(state_ref)

    row = lax.broadcasted_iota(jnp.int32, (c, c), 0)
    col = lax.broadcasted_iota(jnp.int32, (c, c), 1)
    lower = col <= row
    lower_f = jnp.where(lower, 1.0, 0.0)
    eye = jnp.where(col == row, 1.0, 0.0)
    g_rows = g_ref[...]
    gc_rows = _pdot_nt(g_rows, lower_f)
    gc_cols = _pdot_nt(lower_f, g_rows)
    beta_cols = _pdot_nt(eye, beta_ref[...])
    state = state_ref[...]
    for n in range(GDN_CHUNKS_PER_STEP):
        rows = slice(n * c, (n + 1) * c)
        q = q_ref[rows, :]
        k = k_ref[rows, :]
        v = v_ref[rows, :]
        gc_col = gc_cols[:, n:n + 1]
        beta_col = beta_cols[:, n:n + 1]
        decay = jnp.exp(jnp.where(lower, gc_col - gc_rows[n:n + 1, :], NEG_INF))
        kb = k * beta_col
        strict = jnp.where(col < row, _pdot_nt(kb, k) * decay, 0.0)
        inv = _unit_lower_inverse(strict)
        rhs = jnp.concatenate([v * beta_col, kb * jnp.exp(gc_col)], axis=1)
        sol = _pdot(inv, rhs)
        u = sol[:, :HEAD_DIM]
        w = sol[:, HEAD_DIM:]
        a_intra = _pdot_nt(q, k) * decay
        v_new = u - _pdot(w, state)
        o = _pdot(q * jnp.exp(gc_col), state) + _pdot(a_intra, v_new)
        g_last = gc_col[c - 1:c, :]
        state = state * jnp.exp(g_last) + _pdot_tn(k * jnp.exp(g_last - gc_col), v_new)
        z = z_ref[rows, :]
        normed = o * lax.rsqrt(jnp.mean(o * o, axis=-1, keepdims=True) + RMS_EPS) * gain_ref[...]
        o_ref[rows, :] = (normed * (z * _sigmoid(z))).astype(o_ref.dtype)
    state_ref[...] = state


def _gated_delta(qkv, z_src, z_col0, g, beta, norm_out, batch, seq):
    c = DELTA_CHUNK
    tt = GDN_CHUNKS_PER_STEP * c
    steps = seq // tt
    t = batch * seq
    by_head = lambda a: a.T.reshape(DELTA_HEADS, t // c, c)
    tok = lambda off: pl.BlockSpec((tt, HEAD_DIM), lambda b, h, i: (b * steps + i, off + h))
    chunk_rows = pl.BlockSpec((None, GDN_CHUNKS_PER_STEP, c), lambda b, h, i: (h, b * steps + i, 0))
    return pl.pallas_call(
        _gdn_kernel,
        grid=(batch, DELTA_HEADS, steps),
        in_specs=[tok(0), tok(DELTA_HEADS), tok(2 * DELTA_HEADS),
                  pl.BlockSpec((tt, HEAD_DIM), lambda b, h, i: (b * steps + i, z_col0 + h)),
                  chunk_rows, chunk_rows, pl.BlockSpec((1, HEAD_DIM), lambda b, h, i: (0, 0))],
        out_specs=pl.BlockSpec((tt, HEAD_DIM), lambda b, h, i: (b * steps + i, h)),
        out_shape=jax.ShapeDtypeStruct((t, DELTA_WIDTH), BF16),
        scratch_shapes=[pltpu.VMEM((HEAD_DIM, HEAD_DIM), F32)],
        compiler_params=_cparams("parallel", "parallel", "arbitrary"),
        name="gated_delta",
    )(qkv, qkv, qkv, z_src, by_head(g), by_head(beta), norm_out.reshape(1, HEAD_DIM).astype(F32))


SB_WIDTH = 1024
DSA_Q_RANK = 256
DSA_WIDTH = DSA_HEADS * HEAD_DIM


def _stick_pool_mixer(h, norm_g, w_in, pool_w, pool_scale, w_out, batch, seq):
    qkv = _linear([h], [w_in[:, :3 * SB_WIDTH]], norm_g=norm_g, out_dtype=BF16, tn=1024, name="in_ab_qkv")
    u = _linear([h], [w_in[:, 3 * SB_WIDTH:]], norm_g=norm_g, name="in_ab_pool")
    o_a = _sb_attention(qkv, batch, seq, SB_WIDTH // HEAD_DIM)
    o_b = _multiscale_pool(u, pool_w, pool_scale, seq)
    return _linear([o_a, o_b], [w_out[:SB_WIDTH], w_out[SB_WIDTH:]], residual=h, tn=1024, name="out_ab")


def _dsa_delta_mixer(h, norm_g, w_in, w_uq, w_iq, norm_cq, norm_kidx, conv_w, a_log, dt_bias, norm_out, w_out,
                     rel_bias, batch, seq):
    sizes = [DSA_Q_RANK, DSA_WIDTH, DSA_WIDTH, IDX_DIM, IDX_HEADS, 3 * DELTA_WIDTH, DELTA_HEADS, DELTA_HEADS,
             DELTA_WIDTH]
    offs = [0]
    for n in sizes:
        offs.append(offs[-1] + n)
    col = lambda a, b_: w_in[:, offs[a]:offs[b_]]
    pad = jnp.zeros((w_in.shape[0], 128 - (IDX_DIM + IDX_HEADS + 2 * DELTA_HEADS)), w_in.dtype)
    w_f32 = jnp.concatenate([col(5, 6), col(8, 9), col(0, 1), col(3, 5), col(6, 8), pad], axis=1)
    kv = _linear([h], [col(1, 3)], norm_g=norm_g, out_dtype=BF16, tn=1024, name="in_cd_kv")
    cd = _linear([h], [w_f32], norm_g=norm_g, tn=896, name="in_cd_rest")
    z_col0 = 3 * DELTA_WIDTH // HEAD_DIM
    c_q = cd[:, 4 * DELTA_WIDTH:4 * DELTA_WIDTH + DSA_Q_RANK]
    small = cd[:, 4 * DELTA_WIDTH + DSA_Q_RANK:]
    qq = _linear([c_q], [jnp.concatenate([w_uq, w_iq], axis=1)], norm_g=norm_cq, out_dtype=BF16, name="dsa_queries")
    kidx, w_idx, beta, g = _cd_small(small, norm_kidx, a_log, dt_bias)
    mask = _dsa_select(qq, w_idx.T, kidx, batch, seq)
    o_c = _dsa_attention(qq, kv, mask, rel_bias, batch, seq)
    conv = _gdn_conv(cd, conv_w, seq)
    o_d = _gated_delta(conv, cd, z_col0, g, beta, norm_out, batch, seq)
    return _linear([o_c, o_d], [w_out[:DSA_WIDTH], w_out[DSA_WIDTH:]], residual=h, tn=1024, name="out_cd")


def kernel(x, mem, norm_mix, norm_cross, norm_mem, norm_ffn, norm_final, w_in_ab, pool_w, pool_scale, w_out_ab, w_in_cd, w_uq, w_iq, norm_cq, norm_kidx, conv_w, a_log, dt_bias, norm_delta_out, w_out_cd, rel_bias, xattn_wq, xattn_wkv, xattn_wo, peer_wq, peer_subkeys, peer_u, peer_v):
    b, s, d = x.shape
    n_mem = mem.shape[1]
    depth = norm_mix.shape[0]
    h = x.reshape(b * s, d)
    mem2 = mem.reshape(b * n_mem, d)
    for layer in range(depth):
        j = layer // 2
        if layer % 2 == 0:
            h = _stick_pool_mixer(h, norm_mix[layer], w_in_ab[j], pool_w[j], pool_scale[j], w_out_ab[j], b, s)
        else:
            h = _dsa_delta_mixer(h, norm_mix[layer], w_in_cd[j], w_uq[j], w_iq[j], norm_cq[j], norm_kidx[j],
                                 conv_w[j], a_log[j], dt_bias[j], norm_delta_out[j], w_out_cd[j], rel_bias, b, s)
        kv = _linear([mem2], [xattn_wkv[layer]], norm_g=norm_mem[layer], out_dtype=BF16, tm=n_mem, name="xattn_kv")
        h = _cross_attention(h, norm_cross[layer], xattn_wq[layer], kv, xattn_wo[layer], s, n_mem)
        h = _peer_ffn(h, norm_ffn[layer], peer_wq[layer], peer_subkeys[layer], peer_u[layer], peer_v[layer])
    return _rmsnorm(h, norm_final).reshape(b, s, d)
```

```python
import functools
import math

import jax
import jax.numpy as jnp
from jax import lax
from jax.experimental import pallas as pl
from jax.experimental.pallas import tpu as pltpu

F32 = jnp.float32
BF16 = jnp.bfloat16

HEAD_DIM = 128
RMS_EPS = 1e-6
EXP_UNDERFLOW = -104.0
VMEM_LIMIT_BYTES = 56 * 1024 * 1024


def _cparams(*semantics):
    return pltpu.CompilerParams(dimension_semantics=semantics, vmem_limit_bytes=VMEM_LIMIT_BYTES)


def _nt_dot(a, b):
    return lax.dot_general(a, b, (((1,), (1,)), ((), ())), preferred_element_type=F32)


def _dot(a, b):
    return jnp.dot(a, b, preferred_element_type=F32)


def _rms_norm_rows(xf, g):
    return xf * lax.rsqrt(jnp.mean(xf * xf, axis=-1, keepdims=True) + RMS_EPS) * g


def _linear_kernel(*refs, n_lhs, has_norm, has_res):
    pos = 0
    x_refs = refs[pos:pos + n_lhs]; pos += n_lhs
    w_refs = refs[pos:pos + n_lhs]; pos += n_lhs
    g_ref = None
    if has_norm:
        g_ref = refs[pos]; pos += 1
    r_ref = None
    if has_res:
        r_ref = refs[pos]; pos += 1
    o_ref = refs[pos]; pos += 1
    xn_ref = refs[pos] if has_norm else None

    if has_norm:
        @pl.when(pl.program_id(1) == 0)
        def _():
            xn_ref[...] = _rms_norm_rows(x_refs[0][...].astype(F32), g_ref[...]).astype(BF16)
        acc = _dot(xn_ref[...], w_refs[0][...])
    else:
        acc = _dot(x_refs[0][...].astype(BF16), w_refs[0][...])
        for x_ref, w_ref in zip(x_refs[1:], w_refs[1:]):
            acc = acc + _dot(x_ref[...].astype(BF16), w_ref[...])
    if has_res:
        acc = acc + r_ref[...]
    o_ref[...] = acc.astype(o_ref.dtype)


def _linear(xs, ws, *, norm_g=None, residual=None, out_dtype=F32, tm=512, tn=None, name="linear"):
    xs = list(xs)
    ws = [w.astype(BF16) for w in ws]
    t = xs[0].shape[0]
    n = ws[0].shape[1]
    if tn is None:
        tn = n
    assert t % tm == 0 and n % tn == 0
    has_norm = norm_g is not None
    has_res = residual is not None
    assert not has_norm or len(xs) == 1
    in_specs = [pl.BlockSpec((tm, x.shape[1]), lambda i, j: (i, 0)) for x in xs]
    in_specs += [pl.BlockSpec((w.shape[0], tn), lambda i, j: (0, j)) for w in ws]
    args = xs + ws
    if has_norm:
        in_specs.append(pl.BlockSpec((1, xs[0].shape[1]), lambda i, j: (0, 0)))
        args.append(norm_g.reshape(1, -1).astype(F32))
    if has_res:
        in_specs.append(pl.BlockSpec((tm, tn), lambda i, j: (i, j)))
        args.append(residual)
    scratch = [pltpu.VMEM((tm, xs[0].shape[1]), BF16)] if has_norm else []
    return pl.pallas_call(
        functools.partial(_linear_kernel, n_lhs=len(xs), has_norm=has_norm, has_res=has_res),
        grid=(t // tm, n // tn),
        in_specs=in_specs,
        out_specs=pl.BlockSpec((tm, tn), lambda i, j: (i, j)),
        out_shape=jax.ShapeDtypeStruct((t, n), out_dtype),
        scratch_shapes=scratch,
        compiler_params=_cparams("parallel", "arbitrary"),
        name=name,
    )(*args)


SB_BLOCK = 256


def _sb_attn_kernel(q_ref, k_ref, v_ref, o_ref):
    tq = q_ref.shape[0]
    qi = pl.program_id(2)
    scale = HEAD_DIM ** -0.5
    q = q_ref[...]
    row = lax.broadcasted_iota(jnp.int32, (tq, tq), 0)
    col = lax.broadcasted_iota(jnp.int32, (tq, tq), 1)
    suffix = jnp.where(row > col, 1.0, 0.0).astype(BF16)
    before = col < row

    def block(j, carry, acc, diagonal):
        start = pl.multiple_of(j * tq, tq)
        k = k_ref[pl.ds(start, tq), :]
        v = v_ref[pl.ds(start, tq), :]
        z = _nt_dot(q, k) * scale
        lk = -(jnp.maximum(z, 0.0) + jnp.log1p(jnp.exp(-jnp.abs(z))))
        if diagonal:
            lk = jnp.where(before, lk, 0.0)
        lk_hi = lk.astype(BF16)
        lk_lo = (lk - lk_hi.astype(F32)).astype(BF16)
        right = _dot(lk_hi, suffix) + _dot(lk_lo, suffix)
        w = jnp.exp(z + lk + right + carry)
        if diagonal:
            w = jnp.where(before, w, 0.0)
        acc = acc + _dot(w.astype(BF16), v)
        carry = carry + right[:, 0:1] + lk[:, 0:1]
        return carry, acc

    carry, acc = block(qi, jnp.zeros((tq, 1), F32), jnp.zeros((tq, HEAD_DIM), F32), True)

    def cond(state):
        j, carry, _ = state
        return jnp.logical_and(j >= 0, jnp.max(carry) > EXP_UNDERFLOW)

    def body(state):
        j, carry, acc = state
        carry, acc = block(j, carry, acc, False)
        return j - 1, carry, acc

    _, _, acc = lax.while_loop(cond, body, (qi - 1, carry, acc))
    o_ref[...] = acc.astype(o_ref.dtype)


def _sb_attention(qkv, batch, seq, heads):
    tq = SB_BLOCK
    nq = seq // tq
    return pl.pallas_call(
        _sb_attn_kernel,
        grid=(batch, heads, nq),
        in_specs=[
            pl.BlockSpec((tq, HEAD_DIM), lambda b, h, i: (b * nq + i, h)),
            pl.BlockSpec((seq, HEAD_DIM), lambda b, h, i: (b, heads + h)),
            pl.BlockSpec((seq, HEAD_DIM), lambda b, h, i: (b, 2 * heads + h)),
        ],
        out_specs=pl.BlockSpec((tq, HEAD_DIM), lambda b, h, i: (b * nq + i, h)),
        out_shape=jax.ShapeDtypeStruct((batch * seq, heads * HEAD_DIM), BF16),
        compiler_params=_cparams("parallel", "parallel", "arbitrary"),
        name="sb_attention",
    )(qkv, qkv, qkv)


POOL_WINDOWS = (2, 4, 8, 16)
POOL_DIM = 256
POOL_HALO = 16


def _pool_kernel(u_ref, halo_ref, w_ref, scale_ref, o_ref, *, tiles_per_seq):
    tm = u_ref.shape[0]
    tile_in_seq = pl.program_id(0) % tiles_per_seq
    u = u_ref[...]
    halo = jnp.where(tile_in_seq == 0, 0.0, halo_ref[...])
    ext = jnp.concatenate([halo, u], axis=0)
    pos = tile_in_seq * tm + lax.broadcasted_iota(jnp.int32, (tm, 1), 0)
    for g, win in enumerate(POOL_WINDOWS):
        cols = slice(g * POOL_DIM, (g + 1) * POOL_DIM)
        a = ext[:, cols]
        k = 1
        while k < win:
            a = a + pltpu.roll(a, k, 0)
            k *= 2
        count = jnp.minimum(pos + 1, win).astype(F32)
        d = a[POOL_HALO:, :] / count - u[:, cols]
        y = _dot(d.astype(BF16), w_ref[g]) * scale_ref[:, cols]
        o_ref[:, cols] = y.astype(o_ref.dtype)


def _multiscale_pool(u, pool_w, pool_scale, seq, tm=512):
    t, c = u.shape
    assert seq % tm == 0 and tm % POOL_HALO == 0 and max(POOL_WINDOWS) <= POOL_HALO
    halo_blocks_per_tile = tm // POOL_HALO
    return pl.pallas_call(
        functools.partial(_pool_kernel, tiles_per_seq=seq // tm),
        grid=(t // tm,),
        in_specs=[
            pl.BlockSpec((tm, c), lambda i: (i, 0)),
            pl.BlockSpec((POOL_HALO, c), lambda i: (jnp.maximum(i * halo_blocks_per_tile - 1, 0), 0)),
            pl.BlockSpec(pool_w.shape, lambda i: (0, 0, 0)),
            pl.BlockSpec((1, c), lambda i: (0, 0)),
        ],
        out_specs=pl.BlockSpec((tm, c), lambda i: (i, 0)),
        out_shape=jax.ShapeDtypeStruct((t, c), BF16),
        compiler_params=_cparams("parallel"),
        name="multiscale_pool",
    )(u, u, pool_w.astype(BF16), pool_scale.reshape(1, c).astype(F32))


XATTN_HEADS = 4


def _xattn_kernel(h_ref, g_ref, wq_ref, k_ref, v_ref, wo_ref, o_ref):
    scale = HEAD_DIM ** -0.5
    h = h_ref[...]
    hn = _rms_norm_rows(h, g_ref[...]).astype(BF16)
    q = _dot(hn, wq_ref[...]).astype(BF16)
    outs = []
    for hd in range(XATTN_HEADS):
        cols = slice(hd * HEAD_DIM, (hd + 1) * HEAD_DIM)
        logits = _nt_dot(q[:, cols], k_ref[:, cols]) * scale
        logits = logits - jnp.max(logits, axis=-1, keepdims=True)
        e = jnp.exp(logits)
        p = e / jnp.sum(e, axis=-1, keepdims=True)
        outs.append(_dot(p.astype(BF16), v_ref[:, cols]))
    o = jnp.concatenate(outs, axis=-1).astype(BF16)
    o_ref[...] = h + _dot(o, wo_ref[...])


def _cross_attention(h, norm_g, wq, kv, wo, seq, mem_tokens, tm=512):
    t, d = h.shape
    hw = XATTN_HEADS * HEAD_DIM
    tiles_per_seq = seq // tm
    return pl.pallas_call(
        _xattn_kernel,
        grid=(t // tm,),
        in_specs=[
            pl.BlockSpec((tm, d), lambda i: (i, 0)),
            pl.BlockSpec((1, d), lambda i: (0, 0)),
            pl.BlockSpec((d, hw), lambda i: (0, 0)),
            pl.BlockSpec((mem_tokens, hw), lambda i: (i // tiles_per_seq, 0)),
            pl.BlockSpec((mem_tokens, hw), lambda i: (i // tiles_per_seq, 1)),
            pl.BlockSpec((hw, d), lambda i: (0, 0)),
        ],
        out_specs=pl.BlockSpec((tm, d), lambda i: (i, 0)),
        out_shape=jax.ShapeDtypeStruct((t, d), F32),
        compiler_params=_cparams("parallel"),
        name="cross_attention",
    )(h, norm_g.reshape(1, d).astype(F32), wq.astype(BF16), kv, kv, wo.astype(BF16))


def _rmsnorm_kernel(x_ref, g_ref, o_ref):
    o_ref[...] = _rms_norm_rows(x_ref[...], g_ref[...])


def _rmsnorm(x, g, tm=512):
    t, d = x.shape
    return pl.pallas_call(
        _rmsnorm_kernel,
        grid=(t // tm,),
        in_specs=[pl.BlockSpec((tm, d), lambda i: (i, 0)), pl.BlockSpec((1, d), lambda i: (0, 0))],
        out_specs=pl.BlockSpec((tm, d), lambda i: (i, 0)),
        out_shape=jax.ShapeDtypeStruct((t, d), F32),
        compiler_params=_cparams("parallel"),
        name="final_rmsnorm",
    )(x, g.reshape(1, d).astype(F32))


PEER_HEADS = 8
PEER_KEYS = 128
PEER_TOPK = 16
NEG_INF = float("-inf")


def _top16_rows(s, vals_ref):
    rows = lax.broadcasted_iota(jnp.int32, s.shape, 0)
    work = s
    for r in range(PEER_TOPK):
        m = jnp.max(work, axis=0, keepdims=True)
        first = jnp.min(jnp.where(work == m, rows, PEER_KEYS), axis=0, keepdims=True)
        vals_ref[r:r + 1, :] = m
        work = jnp.where(rows == first, NEG_INF, work)
    return work == NEG_INF


def _peer_route_kernel(q_ref, keys_ref, s1_ref, c1_ref, s2_ref, e2_ref, tau_ref, a_ref, b_ref, cand_ref):
    for hd in range(PEER_HEADS):
        s1 = _nt_dot(keys_ref[2 * hd], q_ref[:, (2 * hd) * PEER_KEYS:(2 * hd + 1) * PEER_KEYS])
        s2 = _nt_dot(keys_ref[2 * hd + 1], q_ref[:, (2 * hd + 1) * PEER_KEYS:(2 * hd + 2) * PEER_KEYS])
        top1 = _top16_rows(s1, a_ref)
        top2 = _top16_rows(s2, b_ref)
        a = a_ref[...]
        b = b_ref[...]
        for i in range(PEER_TOPK):
            cand_ref[i * PEER_TOPK:(i + 1) * PEER_TOPK, :] = a[i:i + 1, :] + b
        work = cand_ref[...]
        best = a[0:1, :] + b[0:1, :]
        taken = jnp.zeros_like(best)
        tau = best
        z = jnp.zeros_like(best)
        for _ in range(PEER_TOPK):
            m = jnp.max(work, axis=0, keepdims=True)
            eq = work == m
            cnt = jnp.sum(jnp.where(eq, 1.0, 0.0), axis=0, keepdims=True)
            room = PEER_TOPK - taken
            use = jnp.clip(jnp.minimum(cnt, room), 0.0, None)
            z = z + use * jnp.exp(m - best)
            tau = jnp.where(room > 0.0, m, tau)
            taken = taken + cnt
            work = jnp.where(eq, NEG_INF, work)
        s1_ref[hd] = jnp.where(top1, s1, NEG_INF)
        c1_ref[hd] = jnp.where(top1, jnp.exp(s1 - a[0:1, :]), 0.0) / z
        s2_ref[hd] = jnp.where(top2, s2, NEG_INF)
        e2_ref[hd] = jnp.exp(s2 - b[0:1, :])
        tau_ref[hd:hd + 1, :] = tau


def _peer_route(q, subkeys, tm=256):
    t = q.shape[0]
    keys = subkeys.reshape(PEER_HEADS * 2, PEER_KEYS, PEER_KEYS).astype(BF16)
    big = jax.ShapeDtypeStruct((PEER_HEADS, PEER_KEYS, t), F32)
    big_spec = pl.BlockSpec((PEER_HEADS, PEER_KEYS, tm), lambda i: (0, 0, i))
    return pl.pallas_call(
        _peer_route_kernel,
        grid=(t // tm,),
        in_specs=[
            pl.BlockSpec((tm, q.shape[1]), lambda i: (i, 0)),
            pl.BlockSpec(keys.shape, lambda i: (0, 0, 0)),
        ],
        out_specs=[big_spec, big_spec, big_spec, big_spec, pl.BlockSpec((PEER_HEADS, tm), lambda i: (0, i))],
        out_shape=[big, big, big, big, jax.ShapeDtypeStruct((PEER_HEADS, t), F32)],
        scratch_shapes=[
            pltpu.VMEM((PEER_TOPK, tm), F32),
            pltpu.VMEM((PEER_TOPK, tm), F32),
            pltpu.VMEM((PEER_TOPK * PEER_TOPK, tm), F32),
        ],
        compiler_params=_cparams("parallel"),
        name="peer_route",
    )(q, keys)


PEER_I1_PER_TILE = 8


def _peer_expert_kernel(h_ref, g_ref, u_ref, v_ref, s1_ref, c1_ref, s2_ref, e2_ref, tau_ref, o_ref,
                        xn_ref, acc_ref, act_even_ref, act_odd_ref, *, n_tiles):
    e = pl.program_id(1)
    act_refs = (act_even_ref, act_odd_ref)

    def up_project(slot):
        act_refs[slot][...] = _nt_dot(u_ref[...], xn_ref[...])

    def finish(slot):
        act = act_refs[slot][...]
        act = 0.5 * act * (1.0 + lax.erf(act * (2.0 ** -0.5)))
        gates = []
        for i1 in range(PEER_I1_PER_TILE):
            gate = None
            for hd in range(PEER_HEADS):
                pair = s1_ref[hd, i1:i1 + 1, :] + s2_ref[hd]
                term = c1_ref[hd, i1:i1 + 1, :] * jnp.where(pair >= tau_ref[hd:hd + 1, :], e2_ref[hd], 0.0)
                gate = term if gate is None else gate + term
            gates.append(gate)
        weighted = (jnp.concatenate(gates, axis=0) * act).astype(BF16)
        acc_ref[...] += lax.dot_general(weighted, v_ref[...], (((0,), (0,)), ((), ())),
                                        preferred_element_type=F32)

    @pl.when(e == 0)
    def _():
        xn_ref[...] = _rms_norm_rows(h_ref[...], g_ref[...]).astype(BF16)
        acc_ref[...] = jnp.zeros_like(acc_ref)
        up_project(0)

    middle = jnp.logical_and(e > 0, e < n_tiles)

    @pl.when(jnp.logical_and(middle, e % 2 == 1))
    def _():
        up_project(1)
        finish(0)

    @pl.when(jnp.logical_and(middle, e % 2 == 0))
    def _():
        up_project(0)
        finish(1)

    @pl.when(e == n_tiles)
    def _():
        finish((n_tiles - 1) % 2)
        o_ref[...] = h_ref[...] + acc_ref[...]


def _peer_experts(h, norm_g, u_tab, v_tab, route, tm=512):
    s1, c1, s2, e2, tau = route
    t, d = h.shape
    n_exp = u_tab.shape[0]
    te = PEER_I1_PER_TILE * PEER_KEYS
    n_tiles = n_exp // te
    prev = lambda e: jnp.maximum(e - 1, 0)
    head_rows = pl.BlockSpec((PEER_HEADS, PEER_I1_PER_TILE, tm), lambda i, e: (0, prev(e), i))
    head_full = pl.BlockSpec((PEER_HEADS, PEER_KEYS, tm), lambda i, e: (0, 0, i))
    return pl.pallas_call(
        functools.partial(_peer_expert_kernel, n_tiles=n_tiles),
        grid=(t // tm, n_tiles + 1),
        in_specs=[
            pl.BlockSpec((tm, d), lambda i, e: (i, 0)),
            pl.BlockSpec((1, d), lambda i, e: (0, 0)),
            pl.BlockSpec((te, d), lambda i, e: (jnp.minimum(e, n_tiles - 1), 0)),
            pl.BlockSpec((te, d), lambda i, e: (prev(e), 0)),
            head_rows, head_rows, head_full, head_full,
            pl.BlockSpec((PEER_HEADS, tm), lambda i, e: (0, i)),
        ],
        out_specs=pl.BlockSpec((tm, d), lambda i, e: (i, 0)),
        out_shape=jax.ShapeDtypeStruct((t, d), F32),
        scratch_shapes=[pltpu.VMEM((tm, d), BF16), pltpu.VMEM((tm, d), F32),
                        pltpu.VMEM((te, tm), F32), pltpu.VMEM((te, tm), F32)],
        compiler_params=_cparams("parallel", "arbitrary"),
        name="peer_experts",
    )(h, norm_g.reshape(1, d).astype(F32), u_tab.astype(BF16), v_tab.astype(BF16), s1, c1, s2, e2, tau)


def _peer_ffn(h, norm_g, w_q, subkeys, u_tab, v_tab):
    q = _linear([h], [w_q], norm_g=norm_g, out_dtype=BF16, tn=1024, name="peer_q")
    return _peer_experts(h, norm_g, u_tab, v_tab, _peer_route(q, subkeys))


DSA_HEADS = 8
IDX_HEADS = 16
IDX_DIM = 64
DSA_TOPK = 256
DSA_QUERIES = 256
DSA_SELECT_KEYS = 256
DSA_ATTN_KEYS = 512
DSA_TILE = 128
MASKED = -1e30
REL_BUCKETS = 32
REL_MAX_DIST = 2048
DSA_BIAS_TILES = REL_MAX_DIST // DSA_TILE + 2
INT32_MIN = -2 ** 31


def _dsa_select_kernel(qi_ref, w_ref, kidx_ref, mask_ref, key_ref, *, topk):
    ck = DSA_SELECT_KEYS
    sub = DSA_TILE
    tq = qi_ref.shape[0]
    n_total = mask_ref.shape[0] // ck
    qb = pl.program_id(1)
    n_chunks = (qb + 1) * (tq // ck)
    qi = qi_ref[...]
    w = w_ref[...] * (IDX_HEADS ** -0.5) * (IDX_DIM ** -0.5)
    qpos = qb * tq + lax.broadcasted_iota(jnp.int32, (1, tq), 1)
    krow_sub = lax.broadcasted_iota(jnp.int32, (sub, 1), 0)
    krow = lax.broadcasted_iota(jnp.int32, (ck, 1), 0)

    def score_chunk(c, carry):
        for part in range(ck // sub):
            start = pl.multiple_of(c * ck + part * sub, sub)
            kc = kidx_ref[pl.ds(start, sub), :]
            sc = jnp.zeros((sub, tq), F32)
            for hd in range(IDX_HEADS):
                rel = jnp.maximum(_nt_dot(kc, qi[:, hd * IDX_DIM:(hd + 1) * IDX_DIM]), 0.0)
                sc = sc + w[hd:hd + 1, :] * rel
            sc = jnp.where(start + krow_sub <= qpos, sc + 0.0, NEG_INF)
            bits = pltpu.bitcast(sc, jnp.int32)
            key_ref[pl.ds(start, sub), :] = jnp.where(bits < 0, bits ^ jnp.int32(0x7FFFFFFF), bits)
        return carry

    lax.fori_loop(0, n_chunks, score_chunk, 0)

    def count_ge(cand):
        def chunk(c, acc):
            start = pl.multiple_of(c * ck, ck)
            hit = jnp.where(key_ref[pl.ds(start, ck), :] >= cand, 1, 0)
            return acc + jnp.sum(hit.reshape(ck // 8, 8, tq), axis=0)
        acc = lax.fori_loop(0, n_chunks, chunk, jnp.zeros((8, tq), jnp.int32))
        return jnp.sum(acc, axis=0, keepdims=True)

    def unresolved(state):
        bit, _, held = state
        return jnp.logical_and(bit < 32, jnp.max(held.astype(F32)) > topk)

    def refine(state):
        bit, kth, held = state
        cand = kth + lax.shift_left(jnp.int32(1), 31 - bit)
        cnt = count_ge(cand)
        ok = cnt >= topk
        return bit + 1, jnp.where(ok, cand, kth), jnp.where(ok, cnt, held)

    _, kth, _ = lax.while_loop(
        unresolved, refine,
        (jnp.int32(0), jnp.full((1, tq), INT32_MIN, jnp.int32), jnp.full((1, tq), n_chunks * ck, jnp.int32)))

    def write_mask(c, carry):
        start = pl.multiple_of(c * ck, ck)
        keep = jnp.logical_and(key_ref[pl.ds(start, ck), :] >= kth, start + krow <= qpos)
        mask_ref[pl.ds(start, ck), :] = jnp.where(keep, 0.0, MASKED).astype(mask_ref.dtype)
        return carry

    lax.fori_loop(0, n_chunks, write_mask, 0)

    def write_rest(c, carry):
        start = pl.multiple_of(c * ck, ck)
        mask_ref[pl.ds(start, ck), :] = jnp.full((ck, tq), MASKED, mask_ref.dtype)
        return carry

    lax.fori_loop(n_chunks, n_total, write_rest, 0)


def _dsa_select(qq, w_t, kidx, batch, seq):
    tq = DSA_QUERIES
    assert seq % tq == 0 and tq % DSA_SELECT_KEYS == 0 and DSA_SELECT_KEYS % DSA_TILE == 0
    nq = seq // tq
    topk = min(DSA_TOPK, seq // 4)
    iw = IDX_HEADS * IDX_DIM
    return pl.pallas_call(
        functools.partial(_dsa_select_kernel, topk=topk),
        grid=(batch, nq),
        in_specs=[
            pl.BlockSpec((tq, iw), lambda b, i: (b * nq + i, 1)),
            pl.BlockSpec((IDX_HEADS, tq), lambda b, i: (0, b * nq + i)),
            pl.BlockSpec((seq, IDX_DIM), lambda b, i: (b, 0)),
        ],
        out_specs=pl.BlockSpec((None, seq, tq), lambda b, i: (b, 0, i)),
        out_shape=jax.ShapeDtypeStruct((batch, seq, seq), BF16),
        scratch_shapes=[pltpu.VMEM((seq, tq), jnp.int32)],
        compiler_params=_cparams("parallel", "arbitrary"),
        name="dsa_select",
    )(qq, w_t, kidx)


def _dsa_attn_kernel(q_ref, k_ref, v_ref, mask_ref, bias_ref, o_ref):
    ck = DSA_ATTN_KEYS
    tile = DSA_TILE
    tq = q_ref.shape[0]
    i = pl.program_id(2)
    scale = HEAD_DIM ** -0.5
    q = q_ref[...]
    n_steps = lax.div((i + 1) * tq + (ck - 1), ck)

    def step(c, state):
        m, l, acc = state
        start = pl.multiple_of(c * ck, ck)
        kc = k_ref[pl.ds(start, ck), :]
        vc = v_ref[pl.ds(start, ck), :]
        base = i * (tq // tile) - c * (ck // tile)
        bias = jnp.concatenate(
            [jnp.concatenate([bias_ref[jnp.clip(base + b - a, 0, DSA_BIAS_TILES - 1)]
                              for b in range(tq // tile)], axis=1)
             for a in range(ck // tile)], axis=0)
        s = _nt_dot(kc, q) * scale + bias + mask_ref[pl.ds(start, ck), :].astype(F32)
        m_new = jnp.maximum(m, jnp.max(s, axis=0, keepdims=True))
        alpha = jnp.exp(m - m_new)
        p = jnp.exp(s - m_new)
        l = l * alpha + jnp.sum(p, axis=0, keepdims=True)
        pv = lax.dot_general(vc, p.astype(BF16), (((0,), (0,)), ((), ())), preferred_element_type=F32)
        return m_new, l, acc * alpha + pv

    state = (jnp.full((1, tq), MASKED, F32), jnp.zeros((1, tq), F32), jnp.zeros((HEAD_DIM, tq), F32))
    _, l, acc = lax.fori_loop(0, n_steps, step, state)
    o_ref[...] = (acc / l).T.astype(o_ref.dtype)


def _t5_bucket(dist):
    n = jnp.maximum(dist, 0)
    exact = REL_BUCKETS // 2
    nf = jnp.maximum(n, 1).astype(F32)
    log_ratio = jnp.log(nf / exact) / math.log(REL_MAX_DIST / exact)
    large = exact + (log_ratio * (REL_BUCKETS - exact)).astype(jnp.int32)
    return jnp.where(n < exact, n, jnp.minimum(large, REL_BUCKETS - 1))


def _dsa_bias_tiles(rel_bias):
    tile = DSA_TILE
    heads = rel_bias.shape[1]
    offset = jnp.arange(DSA_BIAS_TILES - 1)[:, None, None] * tile
    bucket = _t5_bucket(offset + jnp.arange(tile)[None, None, :] - jnp.arange(tile)[None, :, None])
    near = jnp.zeros((heads,) + bucket.shape, F32)
    for bkt in range(REL_BUCKETS):
        near = jnp.where(bucket[None] == bkt, rel_bias[bkt].astype(F32)[:, None, None, None], near)
    far = jnp.broadcast_to(rel_bias[REL_BUCKETS - 1].astype(F32)[:, None, None, None], (heads, 1, tile, tile))
    return jnp.concatenate([near, far], axis=1)


def _dsa_attention(qq, kv, mask, rel_bias, batch, seq):
    tq = DSA_QUERIES
    assert seq % DSA_ATTN_KEYS == 0 and tq % DSA_TILE == 0 and DSA_ATTN_KEYS % DSA_TILE == 0
    nq = seq // tq
    return pl.pallas_call(
        _dsa_attn_kernel,
        grid=(batch, DSA_HEADS, nq),
        in_specs=[
            pl.BlockSpec((tq, HEAD_DIM), lambda b, h, i: (b * nq + i, h)),
            pl.BlockSpec((seq, HEAD_DIM), lambda b, h, i: (b, h)),
            pl.BlockSpec((seq, HEAD_DIM), lambda b, h, i: (b, DSA_HEADS + h)),
            pl.BlockSpec((None, seq, tq), lambda b, h, i: (b, 0, i)),
            pl.BlockSpec((None, DSA_BIAS_TILES, DSA_TILE, DSA_TILE), lambda b, h, i: (h, 0, 0, 0)),
        ],
        out_specs=pl.BlockSpec((tq, HEAD_DIM), lambda b, h, i: (b * nq + i, h)),
        out_shape=jax.ShapeDtypeStruct((batch * seq, DSA_HEADS * HEAD_DIM), BF16),
        compiler_params=_cparams("parallel", "parallel", "arbitrary"),
        name="dsa_attention",
    )(qq, kv, kv, mask, _dsa_bias_tiles(rel_bias))


DELTA_HEADS = 8
DELTA_CONV = 4
DELTA_CHUNK = 64
DELTA_WIDTH = DELTA_HEADS * HEAD_DIM
CONV_HALO = 8
L2_EPS = 1e-6
GDN_CHUNKS_PER_STEP = 2


def _softplus(x):
    return jnp.maximum(x, 0.0) + jnp.log1p(jnp.exp(-jnp.abs(x)))


def _sigmoid(x):
    return 1.0 / (1.0 + jnp.exp(-x))


def _cd_small_kernel(x_ref, gk_ref, alog_ref, dt_ref, kidx_ref, w_ref, beta_ref, g_ref):
    x = x_ref[...]
    kidx_ref[...] = _rms_norm_rows(x[:, :IDX_DIM], gk_ref[...]).astype(kidx_ref.dtype)
    o = IDX_DIM
    w_ref[...] = x[:, o:o + IDX_HEADS]
    o += IDX_HEADS
    beta_ref[...] = _sigmoid(x[:, o:o + DELTA_HEADS])
    o += DELTA_HEADS
    g_ref[...] = -jnp.exp(alog_ref[...]) * _softplus(x[:, o:o + DELTA_HEADS] + dt_ref[...])


def _cd_small(small, norm_kidx, a_log, dt_bias, tm=1024):
    t, c = small.shape
    row = lambda n: pl.BlockSpec((1, n), lambda i: (0, 0))
    out = lambda n: pl.BlockSpec((tm, n), lambda i: (i, 0))
    return pl.pallas_call(
        _cd_small_kernel,
        grid=(t // tm,),
        in_specs=[pl.BlockSpec((tm, c), lambda i: (i, 0)), row(IDX_DIM), row(DELTA_HEADS), row(DELTA_HEADS)],
        out_specs=[out(IDX_DIM), out(IDX_HEADS), out(DELTA_HEADS), out(DELTA_HEADS)],
        out_shape=[jax.ShapeDtypeStruct((t, IDX_DIM), BF16), jax.ShapeDtypeStruct((t, IDX_HEADS), F32),
                   jax.ShapeDtypeStruct((t, DELTA_HEADS), F32), jax.ShapeDtypeStruct((t, DELTA_HEADS), F32)],
        compiler_params=_cparams("parallel"),
        name="cd_small_prep",
    )(small, norm_kidx.reshape(1, -1).astype(F32), a_log.reshape(1, -1).astype(F32),
      dt_bias.reshape(1, -1).astype(F32))


def _gdn_conv_kernel(x_ref, halo_ref, w_ref, o_ref, *, tiles_per_seq):
    tm = x_ref.shape[0]
    first = pl.program_id(0) % tiles_per_seq == 0
    ext = jnp.concatenate([jnp.where(first, 0.0, halo_ref[...]), x_ref[...]], axis=0)
    w = w_ref[...]
    y = ext * w[DELTA_CONV - 1:DELTA_CONV, :]
    for back in range(1, DELTA_CONV):
        y = y + pltpu.roll(ext, back, 0) * w[DELTA_CONV - 1 - back:DELTA_CONV - back, :]
    y = y[CONV_HALO:, :]
    y = y * _sigmoid(y)
    for hd in range(3 * DELTA_HEADS):
        cols = slice(hd * HEAD_DIM, (hd + 1) * HEAD_DIM)
        t = y[:, cols]
        if hd < 2 * DELTA_HEADS:
            t = t * lax.rsqrt(jnp.sum(t * t, axis=-1, keepdims=True) + L2_EPS)
        if hd < DELTA_HEADS:
            t = t * (HEAD_DIM ** -0.5)
        o_ref[:, cols] = t


def _gdn_conv(x, conv_w, seq, tm=256):
    t = x.shape[0]
    c = conv_w.shape[1]
    halo_blocks_per_tile = tm // CONV_HALO
    return pl.pallas_call(
        functools.partial(_gdn_conv_kernel, tiles_per_seq=seq // tm),
        grid=(t // tm,),
        in_specs=[
            pl.BlockSpec((tm, c), lambda i: (i, 0)),
            pl.BlockSpec((CONV_HALO, c), lambda i: (jnp.maximum(i * halo_blocks_per_tile - 1, 0), 0)),
            pl.BlockSpec((DELTA_CONV, c), lambda i: (0, 0)),
        ],
        out_specs=pl.BlockSpec((tm, c), lambda i: (i, 0)),
        out_shape=jax.ShapeDtypeStruct((t, c), F32),
        compiler_params=_cparams("parallel"),
        name="gdn_conv",
    )(x, x, conv_w.astype(F32))


def _exact_nt(a, b):
    return lax.dot_general(a, b, (((1,), (1,)), ((), ())), preferred_element_type=F32,
                           precision=lax.Precision.HIGHEST)


def _bdot(a, b):
    return jnp.dot(a.astype(BF16), b.astype(BF16), preferred_element_type=F32)


def _bdot_nt(a, b):
    return _nt_dot(a.astype(BF16), b.astype(BF16))


def _bdot_tn(a, b):
    return lax.dot_general(a.astype(BF16), b.astype(BF16), (((0,), (0,)), ((), ())), preferred_element_type=F32)


def _unit_lower_inverse(a, eye):
    inv = eye - a
    power = a
    span = 2
    while span < a.shape[0]:
        power = _bdot(power, power)
        inv = inv + _bdot(inv, power)
        span *= 2
    return inv


def _gdn_kernel(q_ref, k_ref, v_ref, z_ref, g_ref, beta_ref, gain_ref, o_ref, state_ref):
    c = DELTA_CHUNK

    @pl.when(pl.program_id(1) == 0)
    def _():
        state_ref[...] = jnp.zeros_like(state_ref)

    row = lax.broadcasted_iota(jnp.int32, (c, c), 0)
    col = lax.broadcasted_iota(jnp.int32, (c, c), 1)
    lower = col <= row
    lower_f = jnp.where(lower, 1.0, 0.0)
    eye = jnp.where(col == row, 1.0, 0.0)
    gain = gain_ref[...]
    states = [state_ref[hd] for hd in range(DELTA_HEADS)]
    for n in range(GDN_CHUNKS_PER_STEP):
        rows = slice(n * c, (n + 1) * c)
        g_rows = g_ref[n]
        gc_rows = _exact_nt(g_rows, lower_f)
        gc_cols = _exact_nt(lower_f, g_rows)
        beta_cols = _exact_nt(eye, beta_ref[n])
        local = []
        for hd in range(DELTA_HEADS):
            cols = slice(hd * HEAD_DIM, (hd + 1) * HEAD_DIM)
            q = q_ref[rows, cols]
            k = k_ref[rows, cols]
            v = v_ref[rows, cols]
            gc_col = gc_cols[:, hd:hd + 1]
            beta_col = beta_cols[:, hd:hd + 1]
            decay = jnp.exp(jnp.where(lower, gc_col - gc_rows[hd:hd + 1, :], NEG_INF))
            kb = k * beta_col
            strict = jnp.where(col < row, _bdot_nt(kb, k) * decay, 0.0)
            inv = _unit_lower_inverse(strict, eye)
            sol = _bdot(inv, jnp.concatenate([v * beta_col, kb * jnp.exp(gc_col)], axis=1))
            a_intra = _bdot_nt(q, k) * decay
            g_last = gc_col[c - 1:c, :]
            local.append((sol[:, :HEAD_DIM], sol[:, HEAD_DIM:], a_intra, q * jnp.exp(gc_col),
                          k * jnp.exp(g_last - gc_col), jnp.exp(g_last)))
        for hd in range(DELTA_HEADS):
            cols = slice(hd * HEAD_DIM, (hd + 1) * HEAD_DIM)
            u, w, a_intra, q_decayed, k_decayed, chunk_decay = local[hd]
            v_new = u - _bdot(w, states[hd])
            o = _bdot(q_decayed, states[hd]) + _bdot(a_intra, v_new)
            states[hd] = states[hd] * chunk_decay + _bdot_tn(k_decayed, v_new)
            z = z_ref[rows, cols]
            normed = o * lax.rsqrt(jnp.mean(o * o, axis=-1, keepdims=True) + RMS_EPS) * gain
            o_ref[rows, cols] = (normed * (z * _sigmoid(z))).astype(o_ref.dtype)
    for hd in range(DELTA_HEADS):
        state_ref[hd] = states[hd]


def _gated_delta(qkv, z_src, z_block, g, beta, norm_out, batch, seq):
    c = DELTA_CHUNK
    tt = GDN_CHUNKS_PER_STEP * c
    steps = seq // tt
    t = batch * seq
    by_chunk = lambda a: a.reshape(t // c, c, DELTA_HEADS).transpose(0, 2, 1)
    tok = lambda src_block: pl.BlockSpec((tt, DELTA_WIDTH), lambda b, i: (b * steps + i, src_block))
    chunk_rows = pl.BlockSpec((GDN_CHUNKS_PER_STEP, DELTA_HEADS, c), lambda b, i: (b * steps + i, 0, 0))
    return pl.pallas_call(
        _gdn_kernel,
        grid=(batch, steps),
        in_specs=[tok(0), tok(1), tok(2), tok(z_block), chunk_rows, chunk_rows,
                  pl.BlockSpec((1, HEAD_DIM), lambda b, i: (0, 0))],
        out_specs=pl.BlockSpec((tt, DELTA_WIDTH), lambda b, i: (b * steps + i, 0)),
        out_shape=jax.ShapeDtypeStruct((t, DELTA_WIDTH), BF16),
        scratch_shapes=[pltpu.VMEM((DELTA_HEADS, HEAD_DIM, HEAD_DIM), F32)],
        compiler_params=_cparams("parallel", "arbitrary"),
        name="gated_delta",
    )(qkv, qkv, qkv, z_src, by_chunk(g), by_chunk(beta), norm_out.reshape(1, HEAD_DIM).astype(F32))


SB_WIDTH = 1024
DSA_Q_RANK = 256
DSA_WIDTH = DSA_HEADS * HEAD_DIM


def _stick_pool_mixer(h, norm_g, w_in, pool_w, pool_scale, w_out, batch, seq):
    qkv = _linear([h], [w_in[:, :3 * SB_WIDTH]], norm_g=norm_g, out_dtype=BF16, tn=1024, name="in_ab_qkv")
    u = _linear([h], [w_in[:, 3 * SB_WIDTH:]], norm_g=norm_g, name="in_ab_pool")
    o_a = _sb_attention(qkv, batch, seq, SB_WIDTH // HEAD_DIM)
    o_b = _multiscale_pool(u, pool_w, pool_scale, seq)
    return _linear([o_a, o_b], [w_out[:SB_WIDTH], w_out[SB_WIDTH:]], residual=h, tn=1024, name="out_ab")


def _dsa_delta_mixer(h, norm_g, w_in, w_uq, w_iq, norm_cq, norm_kidx, conv_w, a_log, dt_bias, norm_out, w_out,
                     rel_bias, batch, seq):
    sizes = [DSA_Q_RANK, DSA_WIDTH, DSA_WIDTH, IDX_DIM, IDX_HEADS, 3 * DELTA_WIDTH, DELTA_HEADS, DELTA_HEADS,
             DELTA_WIDTH]
    offs = [0]
    for n in sizes:
        offs.append(offs[-1] + n)
    col = lambda a, b_: w_in[:, offs[a]:offs[b_]]
    pad = jnp.zeros((w_in.shape[0], 128 - (IDX_DIM + IDX_HEADS + 2 * DELTA_HEADS)), w_in.dtype)
    w_f32 = jnp.concatenate([col(5, 6), col(8, 9), col(0, 1), col(3, 5), col(6, 8), pad], axis=1)
    kv = _linear([h], [col(1, 3)], norm_g=norm_g, out_dtype=BF16, tn=1024, name="in_cd_kv")
    cd = _linear([h], [w_f32], norm_g=norm_g, tn=896, name="in_cd_rest")
    z_block = 3
    c_q = cd[:, 4 * DELTA_WIDTH:4 * DELTA_WIDTH + DSA_Q_RANK]
    small = cd[:, 4 * DELTA_WIDTH + DSA_Q_RANK:]
    qq = _linear([c_q], [jnp.concatenate([w_uq, w_iq], axis=1)], norm_g=norm_cq, out_dtype=BF16, name="dsa_queries")
    kidx, w_idx, beta, g = _cd_small(small, norm_kidx, a_log, dt_bias)
    mask = _dsa_select(qq, w_idx.T, kidx, batch, seq)
    o_c = _dsa_attention(qq, kv, mask, rel_bias, batch, seq)
    conv = _gdn_conv(cd, conv_w, seq)
    o_d = _gated_delta(conv, cd, z_block, g, beta, norm_out, batch, seq)
    return _linear([o_c, o_d], [w_out[:DSA_WIDTH], w_out[DSA_WIDTH:]], residual=h, tn=1024, name="out_cd")


def kernel(x, mem, norm_mix, norm_cross, norm_mem, norm_ffn, norm_final, w_in_ab, pool_w, pool_scale, w_out_ab, w_in_cd, w_uq, w_iq, norm_cq, norm_kidx, conv_w, a_log, dt_bias, norm_delta_out, w_out_cd, rel_bias, xattn_wq, xattn_wkv, xattn_wo, peer_wq, peer_subkeys, peer_u, peer_v):
    b, s, d = x.shape
    n_mem = mem.shape[1]
    depth = norm_mix.shape[0]
    h = x.reshape(b * s, d)
    mem2 = mem.reshape(b * n_mem, d)
    for layer in range(depth):
        j = layer // 2
        if layer % 2 == 0:
            h = _stick_pool_mixer(h, norm_mix[layer], w_in_ab[j], pool_w[j], pool_scale[j], w_out_ab[j], b, s)
        else:
            h = _dsa_delta_mixer(h, norm_mix[layer], w_in_cd[j], w_uq[j], w_iq[j], norm_cq[j], norm_kidx[j],
                                 conv_w[j], a_log[j], dt_bias[j], norm_delta_out[j], w_out_cd[j], rel_bias, b, s)
        kv = _linear([mem2], [xattn_wkv[layer]], norm_g=norm_mem[layer], out_dtype=BF16, tm=n_mem, name="xattn_kv")
        h = _cross_attention(h, norm_cross[layer], xattn_wq[layer], kv, xattn_wo[layer], s, n_mem)
        h = _peer_ffn(h, norm_ffn[layer], peer_wq[layer], peer_subkeys[layer], peer_u[layer], peer_v[layer])
    return _rmsnorm(h, norm_final).reshape(b, s, d)
```

```python
import functools
import math

import jax
import jax.numpy as jnp
from jax import lax
from jax.experimental import pallas as pl
from jax.experimental.pallas import tpu as pltpu

F32 = jnp.float32
BF16 = jnp.bfloat16

HEAD_DIM = 128
RMS_EPS = 1e-6
EXP_UNDERFLOW = -104.0
VMEM_LIMIT_BYTES = 56 * 1024 * 1024


def _cparams(*semantics):
    return pltpu.CompilerParams(dimension_semantics=semantics, vmem_limit_bytes=VMEM_LIMIT_BYTES)


def _nt_dot(a, b):
    return lax.dot_general(a, b, (((1,), (1,)), ((), ())), preferred_element_type=F32)


def _dot(a, b):
    return jnp.dot(a, b, preferred_element_type=F32)


def _rms_norm_rows(xf, g):
    return xf * lax.rsqrt(jnp.mean(xf * xf, axis=-1, keepdims=True) + RMS_EPS) * g


def _linear_kernel(*refs, n_lhs, has_norm, has_res):
    pos = 0
    x_refs = refs[pos:pos + n_lhs]; pos += n_lhs
    w_refs = refs[pos:pos + n_lhs]; pos += n_lhs
    g_ref = None
    if has_norm:
        g_ref = refs[pos]; pos += 1
    r_ref = None
    if has_res:
        r_ref = refs[pos]; pos += 1
    o_ref = refs[pos]; pos += 1
    xn_ref = refs[pos] if has_norm else None

    if has_norm:
        @pl.when(pl.program_id(1) == 0)
        def _():
            xn_ref[...] = _rms_norm_rows(x_refs[0][...].astype(F32), g_ref[...]).astype(BF16)
        acc = _dot(xn_ref[...], w_refs[0][...])
    else:
        acc = _dot(x_refs[0][...].astype(BF16), w_refs[0][...])
        for x_ref, w_ref in zip(x_refs[1:], w_refs[1:]):
            acc = acc + _dot(x_ref[...].astype(BF16), w_ref[...])
    if has_res:
        acc = acc + r_ref[...]
    o_ref[...] = acc.astype(o_ref.dtype)


def _linear(xs, ws, *, norm_g=None, residual=None, out_dtype=F32, tm=512, tn=None, name="linear"):
    xs = list(xs)
    ws = [w.astype(BF16) for w in ws]
    t = xs[0].shape[0]
    n = ws[0].shape[1]
    if tn is None:
        tn = n
    assert t % tm == 0 and n % tn == 0
    has_norm = norm_g is not None
    has_res = residual is not None
    assert not has_norm or len(xs) == 1
    in_specs = [pl.BlockSpec((tm, x.shape[1]), lambda i, j: (i, 0)) for x in xs]
    in_specs += [pl.BlockSpec((w.shape[0], tn), lambda i, j: (0, j)) for w in ws]
    args = xs + ws
    if has_norm:
        in_specs.append(pl.BlockSpec((1, xs[0].shape[1]), lambda i, j: (0, 0)))
        args.append(norm_g.reshape(1, -1).astype(F32))
    if has_res:
        in_specs.append(pl.BlockSpec((tm, tn), lambda i, j: (i, j)))
        args.append(residual)
    scratch = [pltpu.VMEM((tm, xs[0].shape[1]), BF16)] if has_norm else []
    return pl.pallas_call(
        functools.partial(_linear_kernel, n_lhs=len(xs), has_norm=has_norm, has_res=has_res),
        grid=(t // tm, n // tn),
        in_specs=in_specs,
        out_specs=pl.BlockSpec((tm, tn), lambda i, j: (i, j)),
        out_shape=jax.ShapeDtypeStruct((t, n), out_dtype),
        scratch_shapes=scratch,
        compiler_params=_cparams("parallel", "arbitrary"),
        name=name,
    )(*args)


SB_BLOCK = 256


def _sb_attn_kernel(q_ref, k_ref, v_ref, o_ref):
    tq = q_ref.shape[0]
    qi = pl.program_id(2)
    scale = HEAD_DIM ** -0.5
    q = q_ref[...]
    row = lax.broadcasted_iota(jnp.int32, (tq, tq), 0)
    col = lax.broadcasted_iota(jnp.int32, (tq, tq), 1)
    suffix = jnp.where(row > col, 1.0, 0.0).astype(BF16)
    before = col < row

    def block(j, carry, acc, diagonal):
        start = pl.multiple_of(j * tq, tq)
        k = k_ref[pl.ds(start, tq), :]
        v = v_ref[pl.ds(start, tq), :]
        z = _nt_dot(q, k) * scale
        lk = -(jnp.maximum(z, 0.0) + jnp.log1p(jnp.exp(-jnp.abs(z))))
        if diagonal:
            lk = jnp.where(before, lk, 0.0)
        lk_hi = lk.astype(BF16)
        lk_lo = (lk - lk_hi.astype(F32)).astype(BF16)
        right = _dot(lk_hi, suffix) + _dot(lk_lo, suffix)
        w = jnp.exp(z + lk + right + carry)
        if diagonal:
            w = jnp.where(before, w, 0.0)
        acc = acc + _dot(w.astype(BF16), v)
        carry = carry + right[:, 0:1] + lk[:, 0:1]
        return carry, acc

    carry, acc = block(qi, jnp.zeros((tq, 1), F32), jnp.zeros((tq, HEAD_DIM), F32), True)

    def cond(state):
        j, carry, _ = state
        return jnp.logical_and(j >= 0, jnp.max(carry) > EXP_UNDERFLOW)

    def body(state):
        j, carry, acc = state
        carry, acc = block(j, carry, acc, False)
        return j - 1, carry, acc

    _, _, acc = lax.while_loop(cond, body, (qi - 1, carry, acc))
    o_ref[...] = acc.astype(o_ref.dtype)


def _sb_attention(qkv, batch, seq, heads):
    tq = SB_BLOCK
    nq = seq // tq
    return pl.pallas_call(
        _sb_attn_kernel,
        grid=(batch, heads, nq),
        in_specs=[
            pl.BlockSpec((tq, HEAD_DIM), lambda b, h, i: (b * nq + i, h)),
            pl.BlockSpec((seq, HEAD_DIM), lambda b, h, i: (b, heads + h)),
            pl.BlockSpec((seq, HEAD_DIM), lambda b, h, i: (b, 2 * heads + h)),
        ],
        out_specs=pl.BlockSpec((tq, HEAD_DIM), lambda b, h, i: (b * nq + i, h)),
        out_shape=jax.ShapeDtypeStruct((batch * seq, heads * HEAD_DIM), BF16),
        compiler_params=_cparams("parallel", "parallel", "arbitrary"),
        name="sb_attention",
    )(qkv, qkv, qkv)


POOL_WINDOWS = (2, 4, 8, 16)
POOL_DIM = 256
POOL_HALO = 16


def _pool_kernel(u_ref, halo_ref, w_ref, scale_ref, o_ref, *, tiles_per_seq):
    tm = u_ref.shape[0]
    tile_in_seq = pl.program_id(0) % tiles_per_seq
    u = u_ref[...]
    halo = jnp.where(tile_in_seq == 0, 0.0, halo_ref[...])
    ext = jnp.concatenate([halo, u], axis=0)
    pos = tile_in_seq * tm + lax.broadcasted_iota(jnp.int32, (tm, 1), 0)
    for g, win in enumerate(POOL_WINDOWS):
        cols = slice(g * POOL_DIM, (g + 1) * POOL_DIM)
        a = ext[:, cols]
        k = 1
        while k < win:
            a = a + pltpu.roll(a, k, 0)
            k *= 2
        count = jnp.minimum(pos + 1, win).astype(F32)
        d = a[POOL_HALO:, :] / count - u[:, cols]
        y = _dot(d.astype(BF16), w_ref[g]) * scale_ref[:, cols]
        o_ref[:, cols] = y.astype(o_ref.dtype)


def _multiscale_pool(u, pool_w, pool_scale, seq, tm=512):
    t, c = u.shape
    assert seq % tm == 0 and tm % POOL_HALO == 0 and max(POOL_WINDOWS) <= POOL_HALO
    halo_blocks_per_tile = tm // POOL_HALO
    return pl.pallas_call(
        functools.partial(_pool_kernel, tiles_per_seq=seq // tm),
        grid=(t // tm,),
        in_specs=[
            pl.BlockSpec((tm, c), lambda i: (i, 0)),
            pl.BlockSpec((POOL_HALO, c), lambda i: (jnp.maximum(i * halo_blocks_per_tile - 1, 0), 0)),
            pl.BlockSpec(pool_w.shape, lambda i: (0, 0, 0)),
            pl.BlockSpec((1, c), lambda i: (0, 0)),
        ],
        out_specs=pl.BlockSpec((tm, c), lambda i: (i, 0)),
        out_shape=jax.ShapeDtypeStruct((t, c), BF16),
        compiler_params=_cparams("parallel"),
        name="multiscale_pool",
    )(u, u, pool_w.astype(BF16), pool_scale.reshape(1, c).astype(F32))


XATTN_HEADS = 4


def _xattn_kernel(h_ref, g_ref, wq_ref, k_ref, v_ref, wo_ref, o_ref):
    scale = HEAD_DIM ** -0.5
    h = h_ref[...]
    hn = _rms_norm_rows(h, g_ref[...]).astype(BF16)
    q = _dot(hn, wq_ref[...]).astype(BF16)
    outs = []
    for hd in range(XATTN_HEADS):
        cols = slice(hd * HEAD_DIM, (hd + 1) * HEAD_DIM)
        logits = _nt_dot(q[:, cols], k_ref[:, cols]) * scale
        logits = logits - jnp.max(logits, axis=-1, keepdims=True)
        e = jnp.exp(logits)
        p = e / jnp.sum(e, axis=-1, keepdims=True)
        outs.append(_dot(p.astype(BF16), v_ref[:, cols]))
    o = jnp.concatenate(outs, axis=-1).astype(BF16)
    o_ref[...] = h + _dot(o, wo_ref[...])


def _cross_attention(h, norm_g, wq, kv, wo, seq, mem_tokens, tm=512):
    t, d = h.shape
    hw = XATTN_HEADS * HEAD_DIM
    tiles_per_seq = seq // tm
    return pl.pallas_call(
        _xattn_kernel,
        grid=(t // tm,),
        in_specs=[
            pl.BlockSpec((tm, d), lambda i: (i, 0)),
            pl.BlockSpec((1, d), lambda i: (0, 0)),
            pl.BlockSpec((d, hw), lambda i: (0, 0)),
            pl.BlockSpec((mem_tokens, hw), lambda i: (i // tiles_per_seq, 0)),
            pl.BlockSpec((mem_tokens, hw), lambda i: (i // tiles_per_seq, 1)),
            pl.BlockSpec((hw, d), lambda i: (0, 0)),
        ],
        out_specs=pl.BlockSpec((tm, d), lambda i: (i, 0)),
        out_shape=jax.ShapeDtypeStruct((t, d), F32),
        compiler_params=_cparams("parallel"),
        name="cross_attention",
    )(h, norm_g.reshape(1, d).astype(F32), wq.astype(BF16), kv, kv, wo.astype(BF16))


def _rmsnorm_kernel(x_ref, g_ref, o_ref):
    o_ref[...] = _rms_norm_rows(x_ref[...], g_ref[...])


def _rmsnorm(x, g, tm=512):
    t, d = x.shape
    return pl.pallas_call(
        _rmsnorm_kernel,
        grid=(t // tm,),
        in_specs=[pl.BlockSpec((tm, d), lambda i: (i, 0)), pl.BlockSpec((1, d), lambda i: (0, 0))],
        out_specs=pl.BlockSpec((tm, d), lambda i: (i, 0)),
        out_shape=jax.ShapeDtypeStruct((t, d), F32),
        compiler_params=_cparams("parallel"),
        name="final_rmsnorm",
    )(x, g.reshape(1, d).astype(F32))


PEER_HEADS = 8
PEER_KEYS = 128
PEER_TOPK = 16
NEG_INF = float("-inf")


def _top16_rows(s, vals_ref):
    rows = lax.broadcasted_iota(jnp.int32, s.shape, 0)
    work = s
    for r in range(PEER_TOPK):
        m = jnp.max(work, axis=0, keepdims=True)
        first = jnp.min(jnp.where(work == m, rows, PEER_KEYS), axis=0, keepdims=True)
        vals_ref[r:r + 1, :] = m
        work = jnp.where(rows == first, NEG_INF, work)
    return work == NEG_INF


def _peer_route_kernel(q_ref, keys_ref, s1_ref, c1_ref, s2_ref, e2_ref, tau_ref, a_ref, b_ref, cand_ref):
    for hd in range(PEER_HEADS):
        s1 = _nt_dot(keys_ref[2 * hd], q_ref[:, (2 * hd) * PEER_KEYS:(2 * hd + 1) * PEER_KEYS])
        s2 = _nt_dot(keys_ref[2 * hd + 1], q_ref[:, (2 * hd + 1) * PEER_KEYS:(2 * hd + 2) * PEER_KEYS])
        top1 = _top16_rows(s1, a_ref)
        top2 = _top16_rows(s2, b_ref)
        a = a_ref[...]
        b = b_ref[...]
        for i in range(PEER_TOPK):
            cand_ref[i * PEER_TOPK:(i + 1) * PEER_TOPK, :] = a[i:i + 1, :] + b
        work = cand_ref[...]
        best = a[0:1, :] + b[0:1, :]
        taken = jnp.zeros_like(best)
        tau = best
        z = jnp.zeros_like(best)
        for _ in range(PEER_TOPK):
            m = jnp.max(work, axis=0, keepdims=True)
            eq = work == m
            cnt = jnp.sum(jnp.where(eq, 1.0, 0.0), axis=0, keepdims=True)
            room = PEER_TOPK - taken
            use = jnp.clip(jnp.minimum(cnt, room), 0.0, None)
            z = z + use * jnp.exp(m - best)
            tau = jnp.where(room > 0.0, m, tau)
            taken = taken + cnt
            work = jnp.where(eq, NEG_INF, work)
        s1_ref[hd] = jnp.where(top1, s1, NEG_INF)
        c1_ref[hd] = jnp.where(top1, jnp.exp(s1 - a[0:1, :]), 0.0) / z
        s2_ref[hd] = jnp.where(top2, s2, NEG_INF)
        e2_ref[hd] = jnp.exp(s2 - b[0:1, :])
        tau_ref[hd:hd + 1, :] = tau


def _peer_route(q, subkeys, tm=256):
    t = q.shape[0]
    keys = subkeys.reshape(PEER_HEADS * 2, PEER_KEYS, PEER_KEYS).astype(BF16)
    big = jax.ShapeDtypeStruct((PEER_HEADS, PEER_KEYS, t), F32)
    big_spec = pl.BlockSpec((PEER_HEADS, PEER_KEYS, tm), lambda i: (0, 0, i))
    return pl.pallas_call(
        _peer_route_kernel,
        grid=(t // tm,),
        in_specs=[
            pl.BlockSpec((tm, q.shape[1]), lambda i: (i, 0)),
            pl.BlockSpec(keys.shape, lambda i: (0, 0, 0)),
        ],
        out_specs=[big_spec, big_spec, big_spec, big_spec, pl.BlockSpec((PEER_HEADS, tm), lambda i: (0, i))],
        out_shape=[big, big, big, big, jax.ShapeDtypeStruct((PEER_HEADS, t), F32)],
        scratch_shapes=[
            pltpu.VMEM((PEER_TOPK, tm), F32),
            pltpu.VMEM((PEER_TOPK, tm), F32),
            pltpu.VMEM((PEER_TOPK * PEER_TOPK, tm), F32),
        ],
        compiler_params=_cparams("parallel"),
        name="peer_route",
    )(q, keys)


PEER_I1_PER_TILE = 8


def _peer_expert_kernel(h_ref, g_ref, u_ref, v_ref, s1_ref, c1_ref, s2_ref, e2_ref, tau_ref, o_ref,
                        xn_ref, acc_ref, act_even_ref, act_odd_ref, *, n_tiles):
    e = pl.program_id(1)
    act_refs = (act_even_ref, act_odd_ref)

    def up_project(slot):
        act_refs[slot][...] = _nt_dot(u_ref[...], xn_ref[...])

    def finish(slot):
        act = act_refs[slot][...]
        act = 0.5 * act * (1.0 + lax.erf(act * (2.0 ** -0.5)))
        gates = []
        for i1 in range(PEER_I1_PER_TILE):
            gate = None
            for hd in range(PEER_HEADS):
                pair = s1_ref[hd, i1:i1 + 1, :] + s2_ref[hd]
                term = c1_ref[hd, i1:i1 + 1, :] * jnp.where(pair >= tau_ref[hd:hd + 1, :], e2_ref[hd], 0.0)
                gate = term if gate is None else gate + term
            gates.append(gate)
        weighted = (jnp.concatenate(gates, axis=0) * act).astype(BF16)
        acc_ref[...] += lax.dot_general(weighted, v_ref[...], (((0,), (0,)), ((), ())),
                                        preferred_element_type=F32)

    @pl.when(e == 0)
    def _():
        xn_ref[...] = _rms_norm_rows(h_ref[...], g_ref[...]).astype(BF16)
        acc_ref[...] = jnp.zeros_like(acc_ref)
        up_project(0)

    middle = jnp.logical_and(e > 0, e < n_tiles)

    @pl.when(jnp.logical_and(middle, e % 2 == 1))
    def _():
        up_project(1)
        finish(0)

    @pl.when(jnp.logical_and(middle, e % 2 == 0))
    def _():
        up_project(0)
        finish(1)

    @pl.when(e == n_tiles)
    def _():
        finish((n_tiles - 1) % 2)
        o_ref[...] = h_ref[...] + acc_ref[...]


def _peer_experts(h, norm_g, u_tab, v_tab, route, tm=512):
    s1, c1, s2, e2, tau = route
    t, d = h.shape
    n_exp = u_tab.shape[0]
    te = PEER_I1_PER_TILE * PEER_KEYS
    n_tiles = n_exp // te
    prev = lambda e: jnp.maximum(e - 1, 0)
    head_rows = pl.BlockSpec((PEER_HEADS, PEER_I1_PER_TILE, tm), lambda i, e: (0, prev(e), i))
    head_full = pl.BlockSpec((PEER_HEADS, PEER_KEYS, tm), lambda i, e: (0, 0, i))
    return pl.pallas_call(
        functools.partial(_peer_expert_kernel, n_tiles=n_tiles),
        grid=(t // tm, n_tiles + 1),
        in_specs=[
            pl.BlockSpec((tm, d), lambda i, e: (i, 0)),
            pl.BlockSpec((1, d), lambda i, e: (0, 0)),
            pl.BlockSpec((te, d), lambda i, e: (jnp.minimum(e, n_tiles - 1), 0)),
            pl.BlockSpec((te, d), lambda i, e: (prev(e), 0)),
            head_rows, head_rows, head_full, head_full,
            pl.BlockSpec((PEER_HEADS, tm), lambda i, e: (0, i)),
        ],
        out_specs=pl.BlockSpec((tm, d), lambda i, e: (i, 0)),
        out_shape=jax.ShapeDtypeStruct((t, d), F32),
        scratch_shapes=[pltpu.VMEM((tm, d), BF16), pltpu.VMEM((tm, d), F32),
                        pltpu.VMEM((te, tm), F32), pltpu.VMEM((te, tm), F32)],
        compiler_params=_cparams("parallel", "arbitrary"),
        name="peer_experts",
    )(h, norm_g.reshape(1, d).astype(F32), u_tab.astype(BF16), v_tab.astype(BF16), s1, c1, s2, e2, tau)


def _peer_ffn(h, norm_g, w_q, subkeys, u_tab, v_tab):
    q = _linear([h], [w_q], norm_g=norm_g, out_dtype=BF16, tn=1024, name="peer_q")
    return _peer_experts(h, norm_g, u_tab, v_tab, _peer_route(q, subkeys))


DSA_HEADS = 8
IDX_HEADS = 16
IDX_DIM = 64
DSA_TOPK = 256
DSA_QUERIES = 256
DSA_SELECT_KEYS = 256
DSA_ATTN_KEYS = 512
DSA_TILE = 128
DSA_HEADS_PER_STEP = 2
MASKED = -1e30
REL_BUCKETS = 32
REL_MAX_DIST = 2048
DSA_BIAS_TILES = REL_MAX_DIST // DSA_TILE + 2
INT32_MIN = -2 ** 31


def _dsa_select_kernel(qi_ref, w_ref, kidx_ref, mask_ref, key_ref, *, topk):
    ck = DSA_SELECT_KEYS
    sub = DSA_TILE
    tq = qi_ref.shape[0]
    n_total = mask_ref.shape[0] // ck
    qb = pl.program_id(1)
    n_chunks = (qb + 1) * (tq // ck)
    qi = qi_ref[...]
    w = w_ref[...] * (IDX_HEADS ** -0.5) * (IDX_DIM ** -0.5)
    qpos = qb * tq + lax.broadcasted_iota(jnp.int32, (1, tq), 1)
    krow_sub = lax.broadcasted_iota(jnp.int32, (sub, 1), 0)
    krow = lax.broadcasted_iota(jnp.int32, (ck, 1), 0)

    def score_chunk(c, carry):
        for part in range(ck // sub):
            start = pl.multiple_of(c * ck + part * sub, sub)
            kc = kidx_ref[pl.ds(start, sub), :]
            sc = jnp.zeros((sub, tq), F32)
            for hd in range(IDX_HEADS):
                rel = jnp.maximum(_nt_dot(kc, qi[:, hd * IDX_DIM:(hd + 1) * IDX_DIM]), 0.0)
                sc = sc + w[hd:hd + 1, :] * rel
            sc = jnp.where(start + krow_sub <= qpos, sc + 0.0, NEG_INF)
            bits = pltpu.bitcast(sc, jnp.int32)
            key_ref[pl.ds(start, sub), :] = jnp.where(bits < 0, bits ^ jnp.int32(0x7FFFFFFF), bits)
        return carry

    lax.fori_loop(0, n_chunks, score_chunk, 0)

    def count_ge(cand):
        def chunk(c, acc):
            start = pl.multiple_of(c * ck, ck)
            hit = jnp.where(key_ref[pl.ds(start, ck), :] >= cand, 1, 0)
            return acc + jnp.sum(hit.reshape(ck // 8, 8, tq), axis=0)
        acc = lax.fori_loop(0, n_chunks, chunk, jnp.zeros((8, tq), jnp.int32))
        return jnp.sum(acc, axis=0, keepdims=True)

    def unresolved(state):
        bit, _, held = state
        return jnp.logical_and(bit < 32, jnp.max(held.astype(F32)) > topk)

    def refine(state):
        bit, kth, held = state
        cand = kth + lax.shift_left(jnp.int32(1), 31 - bit)
        cnt = count_ge(cand)
        ok = cnt >= topk
        return bit + 1, jnp.where(ok, cand, kth), jnp.where(ok, cnt, held)

    _, kth, _ = lax.while_loop(
        unresolved, refine,
        (jnp.int32(0), jnp.full((1, tq), INT32_MIN, jnp.int32), jnp.full((1, tq), n_chunks * ck, jnp.int32)))

    def write_mask(c, carry):
        start = pl.multiple_of(c * ck, ck)
        keep = jnp.logical_and(key_ref[pl.ds(start, ck), :] >= kth, start + krow <= qpos)
        mask_ref[pl.ds(start, ck), :] = jnp.where(keep, 0.0, MASKED).astype(mask_ref.dtype)
        return carry

    lax.fori_loop(0, n_chunks, write_mask, 0)

    def write_rest(c, carry):
        start = pl.multiple_of(c * ck, ck)
        mask_ref[pl.ds(start, ck), :] = jnp.full((ck, tq), MASKED, mask_ref.dtype)
        return carry

    lax.fori_loop(n_chunks, n_total, write_rest, 0)


def _dsa_select(qq, w_t, kidx, batch, seq):
    tq = DSA_QUERIES
    assert seq % tq == 0 and tq % DSA_SELECT_KEYS == 0 and DSA_SELECT_KEYS % DSA_TILE == 0
    nq = seq // tq
    topk = min(DSA_TOPK, seq // 4)
    iw = IDX_HEADS * IDX_DIM
    return pl.pallas_call(
        functools.partial(_dsa_select_kernel, topk=topk),
        grid=(batch, nq),
        in_specs=[
            pl.BlockSpec((tq, iw), lambda b, i: (b * nq + i, 1)),
            pl.BlockSpec((IDX_HEADS, tq), lambda b, i: (0, b * nq + i)),
            pl.BlockSpec((seq, IDX_DIM), lambda b, i: (b, 0)),
        ],
        out_specs=pl.BlockSpec((None, seq, tq), lambda b, i: (b, 0, i)),
        out_shape=jax.ShapeDtypeStruct((batch, seq, seq), BF16),
        scratch_shapes=[pltpu.VMEM((seq, tq), jnp.int32)],
        compiler_params=_cparams("parallel", "arbitrary"),
        name="dsa_select",
    )(qq, w_t, kidx)


def _dsa_attn_kernel(q_ref, k_ref, v_ref, mask_ref, bias_ref, o_ref):
    ck = DSA_ATTN_KEYS
    tile = DSA_TILE
    tq = q_ref.shape[0]
    i = pl.program_id(2)
    scale = HEAD_DIM ** -0.5
    n_steps = lax.div((i + 1) * tq + (ck - 1), ck)

    def step(c, state):
        start = pl.multiple_of(c * ck, ck)
        mask = mask_ref[pl.ds(start, ck), :].astype(F32)
        base = i * (tq // tile) - c * (ck // tile)
        offsets = [[jnp.clip(base + b - a, 0, DSA_BIAS_TILES - 1) for b in range(tq // tile)]
                   for a in range(ck // tile)]
        head_cols = [slice(hd * HEAD_DIM, (hd + 1) * HEAD_DIM) for hd in range(DSA_HEADS_PER_STEP)]
        dots = [_nt_dot(k_ref[pl.ds(start, ck), cols], q_ref[:, cols]) for cols in head_cols]
        new_state = []
        for hd, cols in enumerate(head_cols):
            m, l, acc = state[hd]
            bias = jnp.concatenate(
                [jnp.concatenate([bias_ref[hd, off] for off in row], axis=1) for row in offsets], axis=0)
            s = dots[hd] * scale + bias + mask
            m_new = jnp.maximum(m, jnp.max(s, axis=0, keepdims=True))
            alpha = jnp.exp(m - m_new)
            p = jnp.exp(s - m_new)
            l = l * alpha + jnp.sum(p, axis=0, keepdims=True)
            pv = lax.dot_general(v_ref[pl.ds(start, ck), cols], p.astype(BF16), (((0,), (0,)), ((), ())),
                                 preferred_element_type=F32)
            new_state.append((m_new, l, acc * alpha + pv))
        return tuple(new_state)

    init = (jnp.full((1, tq), MASKED, F32), jnp.zeros((1, tq), F32), jnp.zeros((HEAD_DIM, tq), F32))
    state = lax.fori_loop(0, n_steps, step, (init,) * DSA_HEADS_PER_STEP)
    for hd in range(DSA_HEADS_PER_STEP):
        _, l, acc = state[hd]
        o_ref[:, hd * HEAD_DIM:(hd + 1) * HEAD_DIM] = (acc / l).T.astype(o_ref.dtype)


def _t5_bucket(dist):
    n = jnp.maximum(dist, 0)
    exact = REL_BUCKETS // 2
    nf = jnp.maximum(n, 1).astype(F32)
    log_ratio = jnp.log(nf / exact) / math.log(REL_MAX_DIST / exact)
    large = exact + (log_ratio * (REL_BUCKETS - exact)).astype(jnp.int32)
    return jnp.where(n < exact, n, jnp.minimum(large, REL_BUCKETS - 1))


def _dsa_bias_tiles(rel_bias):
    tile = DSA_TILE
    heads = rel_bias.shape[1]
    offset = jnp.arange(DSA_BIAS_TILES - 1)[:, None, None] * tile
    bucket = _t5_bucket(offset + jnp.arange(tile)[None, None, :] - jnp.arange(tile)[None, :, None])
    near = jnp.zeros((heads,) + bucket.shape, F32)
    for bkt in range(REL_BUCKETS):
        near = jnp.where(bucket[None] == bkt, rel_bias[bkt].astype(F32)[:, None, None, None], near)
    far = jnp.broadcast_to(rel_bias[REL_BUCKETS - 1].astype(F32)[:, None, None, None], (heads, 1, tile, tile))
    return jnp.concatenate([near, far], axis=1)


def _dsa_attention(qq, kv, mask, rel_bias, batch, seq):
    tq = DSA_QUERIES
    assert seq % DSA_ATTN_KEYS == 0 and tq % DSA_TILE == 0 and DSA_ATTN_KEYS % DSA_TILE == 0
    nq = seq // tq
    groups = DSA_HEADS // DSA_HEADS_PER_STEP
    width = DSA_HEADS_PER_STEP * HEAD_DIM
    resident = lambda index_map: pl.BlockSpec((seq, width), index_map, pipeline_mode=pl.Buffered(1))
    return pl.pallas_call(
        _dsa_attn_kernel,
        grid=(batch, groups, nq),
        in_specs=[
            pl.BlockSpec((tq, width), lambda b, h, i: (b * nq + i, h)),
            resident(lambda b, h, i: (b, h)),
            resident(lambda b, h, i: (b, groups + h)),
            pl.BlockSpec((None, seq, tq), lambda b, h, i: (b, 0, i)),
            pl.BlockSpec((DSA_HEADS_PER_STEP, DSA_BIAS_TILES, DSA_TILE, DSA_TILE), lambda b, h, i: (h, 0, 0, 0)),
        ],
        out_specs=pl.BlockSpec((tq, width), lambda b, h, i: (b * nq + i, h)),
        out_shape=jax.ShapeDtypeStruct((batch * seq, DSA_HEADS * HEAD_DIM), BF16),
        compiler_params=_cparams("parallel", "parallel", "arbitrary"),
        name="dsa_attention",
    )(qq, kv, kv, mask, _dsa_bias_tiles(rel_bias))


DELTA_HEADS = 8
DELTA_CONV = 4
DELTA_CHUNK = 64
DELTA_WIDTH = DELTA_HEADS * HEAD_DIM
CONV_HALO = 8
L2_EPS = 1e-6
GDN_CHUNKS_PER_STEP = 2


def _softplus(x):
    return jnp.maximum(x, 0.0) + jnp.log1p(jnp.exp(-jnp.abs(x)))


def _sigmoid(x):
    return 1.0 / (1.0 + jnp.exp(-x))


def _cd_small_kernel(x_ref, gk_ref, alog_ref, dt_ref, kidx_ref, w_ref, beta_ref, g_ref):
    x = x_ref[...]
    kidx_ref[...] = _rms_norm_rows(x[:, :IDX_DIM], gk_ref[...]).astype(kidx_ref.dtype)
    o = IDX_DIM
    w_ref[...] = x[:, o:o + IDX_HEADS]
    o += IDX_HEADS
    beta_ref[...] = _sigmoid(x[:, o:o + DELTA_HEADS])
    o += DELTA_HEADS
    g_ref[...] = -jnp.exp(alog_ref[...]) * _softplus(x[:, o:o + DELTA_HEADS] + dt_ref[...])


def _cd_small(small, norm_kidx, a_log, dt_bias, tm=1024):
    t, c = small.shape
    row = lambda n: pl.BlockSpec((1, n), lambda i: (0, 0))
    out = lambda n: pl.BlockSpec((tm, n), lambda i: (i, 0))
    return pl.pallas_call(
        _cd_small_kernel,
        grid=(t // tm,),
        in_specs=[pl.BlockSpec((tm, c), lambda i: (i, 0)), row(IDX_DIM), row(DELTA_HEADS), row(DELTA_HEADS)],
        out_specs=[out(IDX_DIM), out(IDX_HEADS), out(DELTA_HEADS), out(DELTA_HEADS)],
        out_shape=[jax.ShapeDtypeStruct((t, IDX_DIM), BF16), jax.ShapeDtypeStruct((t, IDX_HEADS), F32),
                   jax.ShapeDtypeStruct((t, DELTA_HEADS), F32), jax.ShapeDtypeStruct((t, DELTA_HEADS), F32)],
        compiler_params=_cparams("parallel"),
        name="cd_small_prep",
    )(small, norm_kidx.reshape(1, -1).astype(F32), a_log.reshape(1, -1).astype(F32),
      dt_bias.reshape(1, -1).astype(F32))


def _gdn_conv_kernel(x_ref, halo_ref, w_ref, o_ref, *, tiles_per_seq):
    tm = x_ref.shape[0]
    first = pl.program_id(0) % tiles_per_seq == 0
    ext = jnp.concatenate([jnp.where(first, 0.0, halo_ref[...]), x_ref[...]], axis=0)
    w = w_ref[...]
    y = ext * w[DELTA_CONV - 1:DELTA_CONV, :]
    for back in range(1, DELTA_CONV):
        y = y + pltpu.roll(ext, back, 0) * w[DELTA_CONV - 1 - back:DELTA_CONV - back, :]
    y = y[CONV_HALO:, :]
    y = y * _sigmoid(y)
    for hd in range(3 * DELTA_HEADS):
        cols = slice(hd * HEAD_DIM, (hd + 1) * HEAD_DIM)
        t = y[:, cols]
        if hd < 2 * DELTA_HEADS:
            t = t * lax.rsqrt(jnp.sum(t * t, axis=-1, keepdims=True) + L2_EPS)
        if hd < DELTA_HEADS:
            t = t * (HEAD_DIM ** -0.5)
        o_ref[:, cols] = t


def _gdn_conv(x, conv_w, seq, tm=256):
    t = x.shape[0]
    c = conv_w.shape[1]
    halo_blocks_per_tile = tm // CONV_HALO
    return pl.pallas_call(
        functools.partial(_gdn_conv_kernel, tiles_per_seq=seq // tm),
        grid=(t // tm,),
        in_specs=[
            pl.BlockSpec((tm, c), lambda i: (i, 0)),
            pl.BlockSpec((CONV_HALO, c), lambda i: (jnp.maximum(i * halo_blocks_per_tile - 1, 0), 0)),
            pl.BlockSpec((DELTA_CONV, c), lambda i: (0, 0)),
        ],
        out_specs=pl.BlockSpec((tm, c), lambda i: (i, 0)),
        out_shape=jax.ShapeDtypeStruct((t, c), F32),
        compiler_params=_cparams("parallel"),
        name="gdn_conv",
    )(x, x, conv_w.astype(F32))


def _exact_nt(a, b):
    return lax.dot_general(a, b, (((1,), (1,)), ((), ())), preferred_element_type=F32,
                           precision=lax.Precision.HIGHEST)


def _bdot(a, b):
    return jnp.dot(a.astype(BF16), b.astype(BF16), preferred_element_type=F32)


def _bdot_nt(a, b):
    return _nt_dot(a.astype(BF16), b.astype(BF16))


def _bdot_tn(a, b):
    return lax.dot_general(a.astype(BF16), b.astype(BF16), (((0,), (0,)), ((), ())), preferred_element_type=F32)


def _unit_lower_inverses(mats, eye):
    invs = [eye - a for a in mats]
    powers = list(mats)
    span = 2
    while span < mats[0].shape[0]:
        powers = [_bdot(p, p) for p in powers]
        invs = [inv + _bdot(inv, p) for inv, p in zip(invs, powers)]
        span *= 2
    return invs


def _gdn_kernel(q_ref, k_ref, v_ref, z_ref, g_ref, beta_ref, gain_ref, o_ref, state_ref):
    c = DELTA_CHUNK

    @pl.when(pl.program_id(1) == 0)
    def _():
        state_ref[...] = jnp.zeros_like(state_ref)

    row = lax.broadcasted_iota(jnp.int32, (c, c), 0)
    col = lax.broadcasted_iota(jnp.int32, (c, c), 1)
    lower = col <= row
    lower_f = jnp.where(lower, 1.0, 0.0)
    eye = jnp.where(col == row, 1.0, 0.0)
    gain = gain_ref[...]
    heads = range(DELTA_HEADS)
    units = [(n, hd) for n in range(GDN_CHUNKS_PER_STEP) for hd in heads]
    block = lambda ref, n, hd: ref[n * c:(n + 1) * c, hd * HEAD_DIM:(hd + 1) * HEAD_DIM]

    gc_rows = [_exact_nt(g_ref[n], lower_f) for n in range(GDN_CHUNKS_PER_STEP)]
    gc_cols = [_exact_nt(lower_f, g_ref[n]) for n in range(GDN_CHUNKS_PER_STEP)]
    beta_cols = [_exact_nt(eye, beta_ref[n]) for n in range(GDN_CHUNKS_PER_STEP)]
    gc_col = {(n, hd): gc_cols[n][:, hd:hd + 1] for n, hd in units}
    beta_col = {(n, hd): beta_cols[n][:, hd:hd + 1] for n, hd in units}
    decay = {u: jnp.exp(jnp.where(lower, gc_col[u] - gc_rows[u[0]][u[1]:u[1] + 1, :], NEG_INF)) for u in units}
    kb = {u: block(k_ref, *u) * beta_col[u] for u in units}
    kk = {u: _bdot_nt(kb[u], block(k_ref, *u)) for u in units}
    qk = {u: _bdot_nt(block(q_ref, *u), block(k_ref, *u)) for u in units}
    invs = _unit_lower_inverses([jnp.where(col < row, kk[u] * decay[u], 0.0) for u in units], eye)
    sol = {u: _bdot(inv, jnp.concatenate([block(v_ref, *u) * beta_col[u], kb[u] * jnp.exp(gc_col[u])], axis=1))
           for u, inv in zip(units, invs)}

    states = [state_ref[hd] for hd in heads]
    for n in range(GDN_CHUNKS_PER_STEP):
        g_last = {hd: gc_col[(n, hd)][c - 1:c, :] for hd in heads}
        both = [_bdot(jnp.concatenate([sol[(n, hd)][:, HEAD_DIM:], block(q_ref, n, hd) * jnp.exp(gc_col[(n, hd)])],
                                      axis=0), states[hd]) for hd in heads]
        v_new = [sol[(n, hd)][:, :HEAD_DIM] - both[hd][:c] for hd in heads]
        intra = [_bdot(qk[(n, hd)] * decay[(n, hd)], v_new[hd]) for hd in heads]
        grow = [_bdot_tn(block(k_ref, n, hd) * jnp.exp(g_last[hd] - gc_col[(n, hd)]), v_new[hd]) for hd in heads]
        for hd in heads:
            states[hd] = states[hd] * jnp.exp(g_last[hd]) + grow[hd]
            o = both[hd][c:] + intra[hd]
            z = block(z_ref, n, hd)
            normed = o * lax.rsqrt(jnp.mean(o * o, axis=-1, keepdims=True) + RMS_EPS) * gain
            o_ref[n * c:(n + 1) * c, hd * HEAD_DIM:(hd + 1) * HEAD_DIM] = (normed * (z * _sigmoid(z))).astype(o_ref.dtype)
    for hd in heads:
        state_ref[hd] = states[hd]


def _gated_delta(qkv, z_src, z_block, g, beta, norm_out, batch, seq):
    c = DELTA_CHUNK
    tt = GDN_CHUNKS_PER_STEP * c
    steps = seq // tt
    t = batch * seq
    by_chunk = lambda a: a.reshape(t // c, c, DELTA_HEADS).transpose(0, 2, 1)
    tok = lambda src_block: pl.BlockSpec((tt, DELTA_WIDTH), lambda b, i: (b * steps + i, src_block))
    chunk_rows = pl.BlockSpec((GDN_CHUNKS_PER_STEP, DELTA_HEADS, c), lambda b, i: (b * steps + i, 0, 0))
    return pl.pallas_call(
        _gdn_kernel,
        grid=(batch, steps),
        in_specs=[tok(0), tok(1), tok(2), tok(z_block), chunk_rows, chunk_rows,
                  pl.BlockSpec((1, HEAD_DIM), lambda b, i: (0, 0))],
        out_specs=pl.BlockSpec((tt, DELTA_WIDTH), lambda b, i: (b * steps + i, 0)),
        out_shape=jax.ShapeDtypeStruct((t, DELTA_WIDTH), BF16),
        scratch_shapes=[pltpu.VMEM((DELTA_HEADS, HEAD_DIM, HEAD_DIM), F32)],
        compiler_params=_cparams("parallel", "arbitrary"),
        name="gated_delta",
    )(qkv, qkv, qkv, z_src, by_chunk(g), by_chunk(beta), norm_out.reshape(1, HEAD_DIM).astype(F32))


SB_WIDTH = 1024
DSA_Q_RANK = 256
DSA_WIDTH = DSA_HEADS * HEAD_DIM


def _stick_pool_mixer(h, norm_g, w_in, pool_w, pool_scale, w_out, batch, seq):
    qkv = _linear([h], [w_in[:, :3 * SB_WIDTH]], norm_g=norm_g, out_dtype=BF16, tn=1024, name="in_ab_qkv")
    u = _linear([h], [w_in[:, 3 * SB_WIDTH:]], norm_g=norm_g, name="in_ab_pool")
    o_a = _sb_attention(qkv, batch, seq, SB_WIDTH // HEAD_DIM)
    o_b = _multiscale_pool(u, pool_w, pool_scale, seq)
    return _linear([o_a, o_b], [w_out[:SB_WIDTH], w_out[SB_WIDTH:]], residual=h, tn=1024, name="out_ab")


def _dsa_delta_mixer(h, norm_g, w_in, w_uq, w_iq, norm_cq, norm_kidx, conv_w, a_log, dt_bias, norm_out, w_out,
                     rel_bias, batch, seq):
    sizes = [DSA_Q_RANK, DSA_WIDTH, DSA_WIDTH, IDX_DIM, IDX_HEADS, 3 * DELTA_WIDTH, DELTA_HEADS, DELTA_HEADS,
             DELTA_WIDTH]
    offs = [0]
    for n in sizes:
        offs.append(offs[-1] + n)
    col = lambda a, b_: w_in[:, offs[a]:offs[b_]]
    pad = jnp.zeros((w_in.shape[0], 128 - (IDX_DIM + IDX_HEADS + 2 * DELTA_HEADS)), w_in.dtype)
    w_f32 = jnp.concatenate([col(5, 6), col(8, 9), col(0, 1), col(3, 5), col(6, 8), pad], axis=1)
    kv = _linear([h], [col(1, 3)], norm_g=norm_g, out_dtype=BF16, tn=1024, name="in_cd_kv")
    cd = _linear([h], [w_f32], norm_g=norm_g, tn=896, name="in_cd_rest")
    z_block = 3
    c_q = cd[:, 4 * DELTA_WIDTH:4 * DELTA_WIDTH + DSA_Q_RANK]
    small = cd[:, 4 * DELTA_WIDTH + DSA_Q_RANK:]
    qq = _linear([c_q], [jnp.concatenate([w_uq, w_iq], axis=1)], norm_g=norm_cq, out_dtype=BF16, name="dsa_queries")
    kidx, w_idx, beta, g = _cd_small(small, norm_kidx, a_log, dt_bias)
    mask = _dsa_select(qq, w_idx.T, kidx, batch, seq)
    o_c = _dsa_attention(qq, kv, mask, rel_bias, batch, seq)
    conv = _gdn_conv(cd, conv_w, seq)
    o_d = _gated_delta(conv, cd, z_block, g, beta, norm_out, batch, seq)
    return _linear([o_c, o_d], [w_out[:DSA_WIDTH], w_out[DSA_WIDTH:]], residual=h, tn=1024, name="out_cd")


def kernel(x, mem, norm_mix, norm_cross, norm_mem, norm_ffn, norm_final, w_in_ab, pool_w, pool_scale, w_out_ab, w_in_cd, w_uq, w_iq, norm_cq, norm_kidx, conv_w, a_log, dt_bias, norm_delta_out, w_out_cd, rel_bias, xattn_wq, xattn_wkv, xattn_wo, peer_wq, peer_subkeys, peer_u, peer_v):
    b, s, d = x.shape
    n_mem = mem.shape[1]
    depth = norm_mix.shape[0]
    h = x.reshape(b * s, d)
    mem2 = mem.reshape(b * n_mem, d)
    for layer in range(depth):
        j = layer // 2
        if layer % 2 == 0:
            h = _stick_pool_mixer(h, norm_mix[layer], w_in_ab[j], pool_w[j], pool_scale[j], w_out_ab[j], b, s)
        else:
            h = _dsa_delta_mixer(h, norm_mix[layer], w_in_cd[j], w_uq[j], w_iq[j], norm_cq[j], norm_kidx[j],
                                 conv_w[j], a_log[j], dt_bias[j], norm_delta_out[j], w_out_cd[j], rel_bias, b, s)
        kv = _linear([mem2], [xattn_wkv[layer]], norm_g=norm_mem[layer], out_dtype=BF16, tm=n_mem, name="xattn_kv")
        h = _cross_attention(h, norm_cross[layer], xattn_wq[layer], kv, xattn_wo[layer], s, n_mem)
        h = _peer_ffn(h, norm_ffn[layer], peer_wq[layer], peer_subkeys[layer], peer_u[layer], peer_v[layer])
    return _rmsnorm(h, norm_final).reshape(b, s, d)
```

```python
import functools
import math

import jax
import jax.numpy as jnp
from jax import lax
from jax.experimental import pallas as pl
from jax.experimental.pallas import tpu as pltpu

F32 = jnp.float32
BF16 = jnp.bfloat16

HEAD_DIM = 128
RMS_EPS = 1e-6
EXP_UNDERFLOW = -104.0
VMEM_LIMIT_BYTES = 56 * 1024 * 1024


def _cparams(*semantics):
    return pltpu.CompilerParams(dimension_semantics=semantics, vmem_limit_bytes=VMEM_LIMIT_BYTES)


def _nt_dot(a, b):
    return lax.dot_general(a, b, (((1,), (1,)), ((), ())), preferred_element_type=F32)


def _dot(a, b):
    return jnp.dot(a, b, preferred_element_type=F32)


def _rms_norm_rows(xf, g):
    return xf * lax.rsqrt(jnp.mean(xf * xf, axis=-1, keepdims=True) + RMS_EPS) * g


def _linear_kernel(*refs, n_lhs, has_norm, has_res):
    pos = 0
    x_refs = refs[pos:pos + n_lhs]; pos += n_lhs
    w_refs = refs[pos:pos + n_lhs]; pos += n_lhs
    g_ref = None
    if has_norm:
        g_ref = refs[pos]; pos += 1
    r_ref = None
    if has_res:
        r_ref = refs[pos]; pos += 1
    o_ref = refs[pos]; pos += 1
    xn_ref = refs[pos] if has_norm else None

    if has_norm:
        @pl.when(pl.program_id(1) == 0)
        def _():
            xn_ref[...] = _rms_norm_rows(x_refs[0][...].astype(F32), g_ref[...]).astype(BF16)
        acc = _dot(xn_ref[...], w_refs[0][...])
    else:
        acc = _dot(x_refs[0][...].astype(BF16), w_refs[0][...])
        for x_ref, w_ref in zip(x_refs[1:], w_refs[1:]):
            acc = acc + _dot(x_ref[...].astype(BF16), w_ref[...])
    if has_res:
        acc = acc + r_ref[...]
    o_ref[...] = acc.astype(o_ref.dtype)


def _linear(xs, ws, *, norm_g=None, residual=None, out_dtype=F32, tm=512, tn=None, name="linear"):
    xs = list(xs)
    ws = [w.astype(BF16) for w in ws]
    t = xs[0].shape[0]
    n = ws[0].shape[1]
    if tn is None:
        tn = n
    assert t % tm == 0 and n % tn == 0
    has_norm = norm_g is not None
    has_res = residual is not None
    assert not has_norm or len(xs) == 1
    in_specs = [pl.BlockSpec((tm, x.shape[1]), lambda i, j: (i, 0)) for x in xs]
    in_specs += [pl.BlockSpec((w.shape[0], tn), lambda i, j: (0, j)) for w in ws]
    args = xs + ws
    if has_norm:
        in_specs.append(pl.BlockSpec((1, xs[0].shape[1]), lambda i, j: (0, 0)))
        args.append(norm_g.reshape(1, -1).astype(F32))
    if has_res:
        in_specs.append(pl.BlockSpec((tm, tn), lambda i, j: (i, j)))
        args.append(residual)
    scratch = [pltpu.VMEM((tm, xs[0].shape[1]), BF16)] if has_norm else []
    return pl.pallas_call(
        functools.partial(_linear_kernel, n_lhs=len(xs), has_norm=has_norm, has_res=has_res),
        grid=(t // tm, n // tn),
        in_specs=in_specs,
        out_specs=pl.BlockSpec((tm, tn), lambda i, j: (i, j)),
        out_shape=jax.ShapeDtypeStruct((t, n), out_dtype),
        scratch_shapes=scratch,
        compiler_params=_cparams("parallel", "arbitrary"),
        name=name,
    )(*args)


SB_BLOCK = 256


SB_HEADS_PER_STEP = 2


def _sb_attn_kernel(q_ref, k_ref, v_ref, o_ref):
    tq = q_ref.shape[0]
    qi = pl.program_id(2)
    scale = HEAD_DIM ** -0.5
    heads = range(SB_HEADS_PER_STEP)
    head_cols = [slice(hd * HEAD_DIM, (hd + 1) * HEAD_DIM) for hd in heads]
    row = lax.broadcasted_iota(jnp.int32, (tq, tq), 0)
    col = lax.broadcasted_iota(jnp.int32, (tq, tq), 1)
    suffix = jnp.where(row > col, 1.0, 0.0).astype(BF16)
    before = col < row

    def block(j, carries, accs, diagonal):
        start = pl.multiple_of(j * tq, tq)
        z = [_nt_dot(q_ref[:, cols], k_ref[pl.ds(start, tq), cols]) * scale for cols in head_cols]
        lk = [-(jnp.maximum(zh, 0.0) + jnp.log1p(jnp.exp(-jnp.abs(zh)))) for zh in z]
        if diagonal:
            lk = [jnp.where(before, t, 0.0) for t in lk]
        lk_hi = [t.astype(BF16) for t in lk]
        lk_lo = [(t - hi.astype(F32)).astype(BF16) for t, hi in zip(lk, lk_hi)]
        right = [_dot(hi, suffix) + _dot(lo, suffix) for hi, lo in zip(lk_hi, lk_lo)]
        w = [jnp.exp(z[hd] + lk[hd] + right[hd] + carries[hd]) for hd in heads]
        if diagonal:
            w = [jnp.where(before, t, 0.0) for t in w]
        accs = tuple(accs[hd] + _dot(w[hd].astype(BF16), v_ref[pl.ds(start, tq), head_cols[hd]]) for hd in heads)
        carries = tuple(carries[hd] + right[hd][:, 0:1] + lk[hd][:, 0:1] for hd in heads)
        return carries, accs

    zeros = lambda width: tuple(jnp.zeros((tq, width), F32) for _ in heads)
    carries, accs = block(qi, zeros(1), zeros(HEAD_DIM), True)

    def cond(state):
        j, carries, _ = state
        live = carries[0]
        for c in carries[1:]:
            live = jnp.maximum(live, c)
        return jnp.logical_and(j >= 0, jnp.max(live) > EXP_UNDERFLOW)

    def body(state):
        j, carries, accs = state
        carries, accs = block(j, carries, accs, False)
        return j - 1, carries, accs

    _, _, accs = lax.while_loop(cond, body, (qi - 1, carries, accs))
    for hd in heads:
        o_ref[:, head_cols[hd]] = accs[hd].astype(o_ref.dtype)


def _sb_attention(qkv, batch, seq, heads):
    tq = SB_BLOCK
    nq = seq // tq
    groups = heads // SB_HEADS_PER_STEP
    width = SB_HEADS_PER_STEP * HEAD_DIM
    resident = lambda index_map: pl.BlockSpec((seq, width), index_map, pipeline_mode=pl.Buffered(1))
    return pl.pallas_call(
        _sb_attn_kernel,
        grid=(batch, groups, nq),
        in_specs=[
            pl.BlockSpec((tq, width), lambda b, h, i: (b * nq + i, h)),
            resident(lambda b, h, i: (b, groups + h)),
            resident(lambda b, h, i: (b, 2 * groups + h)),
        ],
        out_specs=pl.BlockSpec((tq, width), lambda b, h, i: (b * nq + i, h)),
        out_shape=jax.ShapeDtypeStruct((batch * seq, heads * HEAD_DIM), BF16),
        compiler_params=_cparams("parallel", "parallel", "arbitrary"),
        name="sb_attention",
    )(qkv, qkv, qkv)


POOL_WINDOWS = (2, 4, 8, 16)
POOL_DIM = 256
POOL_HALO = 16


def _pool_kernel(u_ref, halo_ref, w_ref, scale_ref, o_ref, *, tiles_per_seq):
    tm = u_ref.shape[0]
    tile_in_seq = pl.program_id(0) % tiles_per_seq
    u = u_ref[...]
    halo = jnp.where(tile_in_seq == 0, 0.0, halo_ref[...])
    ext = jnp.concatenate([halo, u], axis=0)
    pos = tile_in_seq * tm + lax.broadcasted_iota(jnp.int32, (tm, 1), 0)
    for g, win in enumerate(POOL_WINDOWS):
        cols = slice(g * POOL_DIM, (g + 1) * POOL_DIM)
        a = ext[:, cols]
        k = 1
        while k < win:
            a = a + pltpu.roll(a, k, 0)
            k *= 2
        count = jnp.minimum(pos + 1, win).astype(F32)
        d = a[POOL_HALO:, :] / count - u[:, cols]
        y = _dot(d.astype(BF16), w_ref[g]) * scale_ref[:, cols]
        o_ref[:, cols] = y.astype(o_ref.dtype)


def _multiscale_pool(u, pool_w, pool_scale, seq, tm=512):
    t, c = u.shape
    assert seq % tm == 0 and tm % POOL_HALO == 0 and max(POOL_WINDOWS) <= POOL_HALO
    halo_blocks_per_tile = tm // POOL_HALO
    return pl.pallas_call(
        functools.partial(_pool_kernel, tiles_per_seq=seq // tm),
        grid=(t // tm,),
        in_specs=[
            pl.BlockSpec((tm, c), lambda i: (i, 0)),
            pl.BlockSpec((POOL_HALO, c), lambda i: (jnp.maximum(i * halo_blocks_per_tile - 1, 0), 0)),
            pl.BlockSpec(pool_w.shape, lambda i: (0, 0, 0)),
            pl.BlockSpec((1, c), lambda i: (0, 0)),
        ],
        out_specs=pl.BlockSpec((tm, c), lambda i: (i, 0)),
        out_shape=jax.ShapeDtypeStruct((t, c), BF16),
        compiler_params=_cparams("parallel"),
        name="multiscale_pool",
    )(u, u, pool_w.astype(BF16), pool_scale.reshape(1, c).astype(F32))


XATTN_HEADS = 4


def _xattn_kernel(h_ref, g_ref, wq_ref, k_ref, v_ref, wo_ref, o_ref):
    scale = HEAD_DIM ** -0.5
    h = h_ref[...]
    hn = _rms_norm_rows(h, g_ref[...]).astype(BF16)
    q = _dot(hn, wq_ref[...]).astype(BF16)
    outs = []
    for hd in range(XATTN_HEADS):
        cols = slice(hd * HEAD_DIM, (hd + 1) * HEAD_DIM)
        logits = _nt_dot(q[:, cols], k_ref[:, cols]) * scale
        logits = logits - jnp.max(logits, axis=-1, keepdims=True)
        e = jnp.exp(logits)
        p = e / jnp.sum(e, axis=-1, keepdims=True)
        outs.append(_dot(p.astype(BF16), v_ref[:, cols]))
    o = jnp.concatenate(outs, axis=-1).astype(BF16)
    o_ref[...] = h + _dot(o, wo_ref[...])


def _cross_attention(h, norm_g, wq, kv, wo, seq, mem_tokens, tm=512):
    t, d = h.shape
    hw = XATTN_HEADS * HEAD_DIM
    tiles_per_seq = seq // tm
    return pl.pallas_call(
        _xattn_kernel,
        grid=(t // tm,),
        in_specs=[
            pl.BlockSpec((tm, d), lambda i: (i, 0)),
            pl.BlockSpec((1, d), lambda i: (0, 0)),
            pl.BlockSpec((d, hw), lambda i: (0, 0)),
            pl.BlockSpec((mem_tokens, hw), lambda i: (i // tiles_per_seq, 0)),
            pl.BlockSpec((mem_tokens, hw), lambda i: (i // tiles_per_seq, 1)),
            pl.BlockSpec((hw, d), lambda i: (0, 0)),
        ],
        out_specs=pl.BlockSpec((tm, d), lambda i: (i, 0)),
        out_shape=jax.ShapeDtypeStruct((t, d), F32),
        compiler_params=_cparams("parallel"),
        name="cross_attention",
    )(h, norm_g.reshape(1, d).astype(F32), wq.astype(BF16), kv, kv, wo.astype(BF16))


def _rmsnorm_kernel(x_ref, g_ref, o_ref):
    o_ref[...] = _rms_norm_rows(x_ref[...], g_ref[...])


def _rmsnorm(x, g, tm=512):
    t, d = x.shape
    return pl.pallas_call(
        _rmsnorm_kernel,
        grid=(t // tm,),
        in_specs=[pl.BlockSpec((tm, d), lambda i: (i, 0)), pl.BlockSpec((1, d), lambda i: (0, 0))],
        out_specs=pl.BlockSpec((tm, d), lambda i: (i, 0)),
        out_shape=jax.ShapeDtypeStruct((t, d), F32),
        compiler_params=_cparams("parallel"),
        name="final_rmsnorm",
    )(x, g.reshape(1, d).astype(F32))


PEER_HEADS = 8
PEER_KEYS = 128
PEER_TOPK = 16
NEG_INF = float("-inf")


def _top16_rows(s, vals_ref, idx_ref=None):
    rows = lax.broadcasted_iota(jnp.int32, s.shape, 0)
    work = s
    for r in range(PEER_TOPK):
        m = jnp.max(work, axis=0, keepdims=True)
        first = jnp.min(jnp.where(work == m, rows, PEER_KEYS), axis=0, keepdims=True)
        vals_ref[r:r + 1, :] = m
        if idx_ref is not None:
            idx_ref[r:r + 1, :] = first
        work = jnp.where(rows == first, NEG_INF, work)
    return work == NEG_INF


def _peer_route_kernel(q_ref, keys_ref, thr_ref, c1_ref, s2_ref, e2_ref, a_ref, b_ref, cand_ref, idx_ref):
    rows = lax.broadcasted_iota(jnp.int32, (PEER_KEYS, q_ref.shape[0]), 0)
    for hd in range(PEER_HEADS):
        s1 = _nt_dot(keys_ref[2 * hd], q_ref[:, (2 * hd) * PEER_KEYS:(2 * hd + 1) * PEER_KEYS])
        s2 = _nt_dot(keys_ref[2 * hd + 1], q_ref[:, (2 * hd + 1) * PEER_KEYS:(2 * hd + 2) * PEER_KEYS])
        top1 = _top16_rows(s1, a_ref, idx_ref)
        top2 = _top16_rows(s2, b_ref)
        a = a_ref[...]
        b = b_ref[...]
        for i in range(PEER_TOPK):
            cand_ref[i * PEER_TOPK:(i + 1) * PEER_TOPK, :] = a[i:i + 1, :] + b
        work = cand_ref[...]
        best = a[0:1, :] + b[0:1, :]
        taken = jnp.zeros_like(best)
        tau = best
        z = jnp.zeros_like(best)
        for _ in range(PEER_TOPK):
            m = jnp.max(work, axis=0, keepdims=True)
            eq = work == m
            cnt = jnp.sum(jnp.where(eq, 1.0, 0.0), axis=0, keepdims=True)
            room = PEER_TOPK - taken
            use = jnp.clip(jnp.minimum(cnt, room), 0.0, None)
            z = z + use * jnp.exp(m - best)
            tau = jnp.where(room > 0.0, m, tau)
            taken = taken + cnt
            work = jnp.where(eq, NEG_INF, work)
        idx = idx_ref[...]
        thr = jnp.full(s1.shape, jnp.inf, F32)
        for r in range(PEER_TOPK):
            pair_r = cand_ref[r * PEER_TOPK:(r + 1) * PEER_TOPK, :]
            thr_r = jnp.min(jnp.where(pair_r >= tau, b, jnp.inf), axis=0, keepdims=True)
            thr = jnp.where(rows == idx[r:r + 1, :], thr_r, thr)
        thr_ref[hd] = thr
        c1_ref[hd] = jnp.where(top1, jnp.exp(s1 - a[0:1, :]), 0.0) / z
        s2_ref[hd] = jnp.where(top2, s2, NEG_INF)
        e2_ref[hd] = jnp.exp(s2 - b[0:1, :])


def _peer_route(q, subkeys, tm=256):
    t = q.shape[0]
    keys = subkeys.reshape(PEER_HEADS * 2, PEER_KEYS, PEER_KEYS).astype(BF16)
    big = jax.ShapeDtypeStruct((PEER_HEADS, PEER_KEYS, t), F32)
    big_spec = pl.BlockSpec((PEER_HEADS, PEER_KEYS, tm), lambda i: (0, 0, i))
    return pl.pallas_call(
        _peer_route_kernel,
        grid=(t // tm,),
        in_specs=[
            pl.BlockSpec((tm, q.shape[1]), lambda i: (i, 0)),
            pl.BlockSpec(keys.shape, lambda i: (0, 0, 0)),
        ],
        out_specs=[big_spec, big_spec, big_spec, big_spec],
        out_shape=[big, big, big, big],
        scratch_shapes=[
            pltpu.VMEM((PEER_TOPK, tm), F32),
            pltpu.VMEM((PEER_TOPK, tm), F32),
            pltpu.VMEM((PEER_TOPK * PEER_TOPK, tm), F32),
            pltpu.VMEM((PEER_TOPK, tm), jnp.int32),
        ],
        compiler_params=_cparams("parallel"),
        name="peer_route",
    )(q, keys)


PEER_I1_PER_TILE = 8


def _peer_expert_kernel(h_ref, g_ref, u_ref, v_ref, thr_ref, c1_ref, s2_ref, e2_ref, o_ref,
                        xn_ref, acc_ref, act_even_ref, act_odd_ref, *, n_tiles):
    e = pl.program_id(1)
    act_refs = (act_even_ref, act_odd_ref)

    def up_project(slot):
        act_refs[slot][...] = _nt_dot(u_ref[...], xn_ref[...])

    def finish(slot):
        act = act_refs[slot][...]
        act = 0.5 * act * (1.0 + lax.erf(act * (2.0 ** -0.5)))
        gates = []
        for i1 in range(PEER_I1_PER_TILE):
            gate = None
            for hd in range(PEER_HEADS):
                kept = jnp.where(s2_ref[hd] >= thr_ref[hd, i1:i1 + 1, :], e2_ref[hd], 0.0)
                term = c1_ref[hd, i1:i1 + 1, :] * kept
                gate = term if gate is None else gate + term
            gates.append(gate)
        weighted = (jnp.concatenate(gates, axis=0) * act).astype(BF16)
        acc_ref[...] += lax.dot_general(weighted, v_ref[...], (((0,), (0,)), ((), ())),
                                        preferred_element_type=F32)

    @pl.when(e == 0)
    def _():
        xn_ref[...] = _rms_norm_rows(h_ref[...], g_ref[...]).astype(BF16)
        acc_ref[...] = jnp.zeros_like(acc_ref)
        up_project(0)

    middle = jnp.logical_and(e > 0, e < n_tiles)

    @pl.when(jnp.logical_and(middle, e % 2 == 1))
    def _():
        up_project(1)
        finish(0)

    @pl.when(jnp.logical_and(middle, e % 2 == 0))
    def _():
        up_project(0)
        finish(1)

    @pl.when(e == n_tiles)
    def _():
        finish((n_tiles - 1) % 2)
        o_ref[...] = h_ref[...] + acc_ref[...]


def _peer_experts(h, norm_g, u_tab, v_tab, route, tm=512):
    thr, c1, s2, e2 = route
    t, d = h.shape
    n_exp = u_tab.shape[0]
    te = PEER_I1_PER_TILE * PEER_KEYS
    n_tiles = n_exp // te
    prev = lambda e: jnp.maximum(e - 1, 0)
    head_rows = pl.BlockSpec((PEER_HEADS, PEER_I1_PER_TILE, tm), lambda i, e: (0, prev(e), i))
    head_full = pl.BlockSpec((PEER_HEADS, PEER_KEYS, tm), lambda i, e: (0, 0, i))
    return pl.pallas_call(
        functools.partial(_peer_expert_kernel, n_tiles=n_tiles),
        grid=(t // tm, n_tiles + 1),
        in_specs=[
            pl.BlockSpec((tm, d), lambda i, e: (i, 0)),
            pl.BlockSpec((1, d), lambda i, e: (0, 0)),
            pl.BlockSpec((te, d), lambda i, e: (jnp.minimum(e, n_tiles - 1), 0)),
            pl.BlockSpec((te, d), lambda i, e: (prev(e), 0)),
            head_rows, head_rows, head_full, head_full,
        ],
        out_specs=pl.BlockSpec((tm, d), lambda i, e: (i, 0)),
        out_shape=jax.ShapeDtypeStruct((t, d), F32),
        scratch_shapes=[pltpu.VMEM((tm, d), BF16), pltpu.VMEM((tm, d), F32),
                        pltpu.VMEM((te, tm), F32), pltpu.VMEM((te, tm), F32)],
        compiler_params=_cparams("parallel", "arbitrary"),
        name="peer_experts",
    )(h, norm_g.reshape(1, d).astype(F32), u_tab.astype(BF16), v_tab.astype(BF16), thr, c1, s2, e2)


def _peer_ffn(h, norm_g, w_q, subkeys, u_tab, v_tab):
    q = _linear([h], [w_q], norm_g=norm_g, out_dtype=BF16, tn=1024, name="peer_q")
    return _peer_experts(h, norm_g, u_tab, v_tab, _peer_route(q, subkeys))


DSA_HEADS = 8
IDX_HEADS = 16
IDX_DIM = 64
DSA_TOPK = 256
DSA_QUERIES = 256
DSA_SELECT_KEYS = 256
DSA_ATTN_KEYS = 512
DSA_TILE = 128
DSA_HEADS_PER_STEP = 2
MASKED = -1e30
REL_BUCKETS = 32
REL_MAX_DIST = 2048
DSA_BIAS_TILES = REL_MAX_DIST // DSA_TILE + 2
INT32_MIN = -2 ** 31


def _dsa_select_kernel(qi_ref, w_ref, kidx_ref, mask_ref, key_ref, *, topk):
    ck = DSA_SELECT_KEYS
    sub = DSA_TILE
    tq = qi_ref.shape[0]
    n_total = mask_ref.shape[0] // ck
    qb = pl.program_id(1)
    n_chunks = (qb + 1) * (tq // ck)
    qi = qi_ref[...]
    w = w_ref[...] * (IDX_HEADS ** -0.5) * (IDX_DIM ** -0.5)
    qpos = qb * tq + lax.broadcasted_iota(jnp.int32, (1, tq), 1)
    krow_sub = lax.broadcasted_iota(jnp.int32, (sub, 1), 0)
    krow = lax.broadcasted_iota(jnp.int32, (ck, 1), 0)

    def score_chunk(c, carry):
        for part in range(ck // sub):
            start = pl.multiple_of(c * ck + part * sub, sub)
            kc = kidx_ref[pl.ds(start, sub), :]
            sc = jnp.zeros((sub, tq), F32)
            for hd in range(IDX_HEADS):
                rel = jnp.maximum(_nt_dot(kc, qi[:, hd * IDX_DIM:(hd + 1) * IDX_DIM]), 0.0)
                sc = sc + w[hd:hd + 1, :] * rel
            sc = jnp.where(start + krow_sub <= qpos, sc + 0.0, NEG_INF)
            bits = pltpu.bitcast(sc, jnp.int32)
            key_ref[pl.ds(start, sub), :] = jnp.where(bits < 0, bits ^ jnp.int32(0x7FFFFFFF), bits)
        return carry

    lax.fori_loop(0, n_chunks, score_chunk, 0)

    @pl.when(n_chunks % 2 == 1)
    def _():
        key_ref[pl.ds(pl.multiple_of(n_chunks * ck, ck), ck), :] = jnp.full((ck, tq), INT32_MIN, jnp.int32)

    def count_ge(cand):
        def chunk_pair(c, acc):
            start = pl.multiple_of(c * (2 * ck), 2 * ck)
            hit = jnp.where(key_ref[pl.ds(start, 2 * ck), :] >= cand, 1, 0)
            return acc + jnp.sum(hit.reshape(2 * ck // 8, 8, tq), axis=0)
        n_pairs = lax.shift_right_logical(n_chunks + 1, 1)
        acc = lax.fori_loop(0, n_pairs, chunk_pair, jnp.zeros((8, tq), jnp.int32))
        return jnp.sum(acc, axis=0, keepdims=True)

    def unresolved(state):
        bit, _, held = state
        return jnp.logical_and(bit < 32, jnp.max(held.astype(F32)) > topk)

    def refine(state):
        bit, kth, held = state
        cand = kth + lax.shift_left(jnp.int32(1), 31 - bit)
        cnt = count_ge(cand)
        ok = cnt >= topk
        return bit + 1, jnp.where(ok, cand, kth), jnp.where(ok, cnt, held)

    _, kth, _ = lax.while_loop(
        unresolved, refine,
        (jnp.int32(0), jnp.full((1, tq), INT32_MIN, jnp.int32), jnp.full((1, tq), n_chunks * ck, jnp.int32)))

    def write_mask(c, carry):
        start = pl.multiple_of(c * ck, ck)
        keep = jnp.logical_and(key_ref[pl.ds(start, ck), :] >= kth, start + krow <= qpos)
        mask_ref[pl.ds(start, ck), :] = jnp.where(keep, 0.0, MASKED).astype(mask_ref.dtype)
        return carry

    lax.fori_loop(0, n_chunks, write_mask, 0)

    def write_rest(c, carry):
        start = pl.multiple_of(c * ck, ck)
        mask_ref[pl.ds(start, ck), :] = jnp.full((ck, tq), MASKED, mask_ref.dtype)
        return carry

    lax.fori_loop(n_chunks, n_total, write_rest, 0)


def _dsa_select(qq, w_t, kidx, batch, seq):
    tq = DSA_QUERIES
    assert seq % (2 * DSA_SELECT_KEYS) == 0 and tq == DSA_SELECT_KEYS and DSA_SELECT_KEYS % DSA_TILE == 0
    nq = seq // tq
    topk = min(DSA_TOPK, seq // 4)
    iw = IDX_HEADS * IDX_DIM
    return pl.pallas_call(
        functools.partial(_dsa_select_kernel, topk=topk),
        grid=(batch, nq),
        in_specs=[
            pl.BlockSpec((tq, iw), lambda b, i: (b * nq + i, 1)),
            pl.BlockSpec((IDX_HEADS, tq), lambda b, i: (0, b * nq + i)),
            pl.BlockSpec((seq, IDX_DIM), lambda b, i: (b, 0)),
        ],
        out_specs=pl.BlockSpec((None, seq, tq), lambda b, i: (b, 0, i)),
        out_shape=jax.ShapeDtypeStruct((batch, seq, seq), BF16),
        scratch_shapes=[pltpu.VMEM((seq, tq), jnp.int32)],
        compiler_params=_cparams("parallel", "arbitrary"),
        name="dsa_select",
    )(qq, w_t, kidx)


def _dsa_attn_kernel(q_ref, k_ref, v_ref, mask_ref, bias_ref, o_ref):
    ck = DSA_ATTN_KEYS
    tile = DSA_TILE
    tq = q_ref.shape[0]
    i = pl.program_id(2)
    scale = HEAD_DIM ** -0.5
    n_steps = lax.div((i + 1) * tq + (ck - 1), ck)

    def step(c, state):
        start = pl.multiple_of(c * ck, ck)
        mask = mask_ref[pl.ds(start, ck), :].astype(F32)
        base = i * (tq // tile) - c * (ck // tile)
        offsets = [[jnp.clip(base + b - a, 0, DSA_BIAS_TILES - 1) for b in range(tq // tile)]
                   for a in range(ck // tile)]
        head_cols = [slice(hd * HEAD_DIM, (hd + 1) * HEAD_DIM) for hd in range(DSA_HEADS_PER_STEP)]
        dots = [_nt_dot(k_ref[pl.ds(start, ck), cols], q_ref[:, cols]) for cols in head_cols]
        new_state = []
        for hd, cols in enumerate(head_cols):
            m, l, acc = state[hd]
            bias = jnp.concatenate(
                [jnp.concatenate([bias_ref[hd, off] for off in row], axis=1) for row in offsets], axis=0)
            s = dots[hd] * scale + bias + mask
            m_new = jnp.maximum(m, jnp.max(s, axis=0, keepdims=True))
            alpha = jnp.exp(m - m_new)
            p = jnp.exp(s - m_new)
            l = l * alpha + jnp.sum(p, axis=0, keepdims=True)
            pv = lax.dot_general(v_ref[pl.ds(start, ck), cols], p.astype(BF16), (((0,), (0,)), ((), ())),
                                 preferred_element_type=F32)
            new_state.append((m_new, l, acc * alpha + pv))
        return tuple(new_state)

    init = (jnp.full((1, tq), MASKED, F32), jnp.zeros((1, tq), F32), jnp.zeros((HEAD_DIM, tq), F32))
    state = lax.fori_loop(0, n_steps, step, (init,) * DSA_HEADS_PER_STEP)
    for hd in range(DSA_HEADS_PER_STEP):
        _, l, acc = state[hd]
        o_ref[:, hd * HEAD_DIM:(hd + 1) * HEAD_DIM] = (acc / l).T.astype(o_ref.dtype)


def _t5_bucket(dist):
    n = jnp.maximum(dist, 0)
    exact = REL_BUCKETS // 2
    nf = jnp.maximum(n, 1).astype(F32)
    log_ratio = jnp.log(nf / exact) / math.log(REL_MAX_DIST / exact)
    large = exact + (log_ratio * (REL_BUCKETS - exact)).astype(jnp.int32)
    return jnp.where(n < exact, n, jnp.minimum(large, REL_BUCKETS - 1))


def _dsa_bias_tiles(rel_bias):
    tile = DSA_TILE
    heads = rel_bias.shape[1]
    offset = jnp.arange(DSA_BIAS_TILES - 1)[:, None, None] * tile
    bucket = _t5_bucket(offset + jnp.arange(tile)[None, None, :] - jnp.arange(tile)[None, :, None])
    near = jnp.zeros((heads,) + bucket.shape, F32)
    for bkt in range(REL_BUCKETS):
        near = jnp.where(bucket[None] == bkt, rel_bias[bkt].astype(F32)[:, None, None, None], near)
    far = jnp.broadcast_to(rel_bias[REL_BUCKETS - 1].astype(F32)[:, None, None, None], (heads, 1, tile, tile))
    return jnp.concatenate([near, far], axis=1)


def _dsa_attention(qq, kv, mask, rel_bias, batch, seq):
    tq = DSA_QUERIES
    assert seq % DSA_ATTN_KEYS == 0 and tq % DSA_TILE == 0 and DSA_ATTN_KEYS % DSA_TILE == 0
    nq = seq // tq
    groups = DSA_HEADS // DSA_HEADS_PER_STEP
    width = DSA_HEADS_PER_STEP * HEAD_DIM
    resident = lambda index_map: pl.BlockSpec((seq, width), index_map, pipeline_mode=pl.Buffered(1))
    return pl.pallas_call(
        _dsa_attn_kernel,
        grid=(batch, groups, nq),
        in_specs=[
            pl.BlockSpec((tq, width), lambda b, h, i: (b * nq + i, h)),
            resident(lambda b, h, i: (b, h)),
            resident(lambda b, h, i: (b, groups + h)),
            pl.BlockSpec((None, seq, tq), lambda b, h, i: (b, 0, i)),
            pl.BlockSpec((DSA_HEADS_PER_STEP, DSA_BIAS_TILES, DSA_TILE, DSA_TILE), lambda b, h, i: (h, 0, 0, 0)),
        ],
        out_specs=pl.BlockSpec((tq, width), lambda b, h, i: (b * nq + i, h)),
        out_shape=jax.ShapeDtypeStruct((batch * seq, DSA_HEADS * HEAD_DIM), BF16),
        compiler_params=_cparams("parallel", "parallel", "arbitrary"),
        name="dsa_attention",
    )(qq, kv, kv, mask, _dsa_bias_tiles(rel_bias))


DELTA_HEADS = 8
DELTA_CONV = 4
DELTA_CHUNK = 64
DELTA_WIDTH = DELTA_HEADS * HEAD_DIM
CONV_HALO = 8
L2_EPS = 1e-6
GDN_CHUNKS_PER_STEP = 2


def _softplus(x):
    return jnp.maximum(x, 0.0) + jnp.log1p(jnp.exp(-jnp.abs(x)))


def _sigmoid(x):
    return 1.0 / (1.0 + jnp.exp(-x))


def _cd_small_kernel(x_ref, gk_ref, alog_ref, dt_ref, kidx_ref, w_ref, beta_ref, g_ref):
    x = x_ref[...]
    kidx_ref[...] = _rms_norm_rows(x[:, :IDX_DIM], gk_ref[...]).astype(kidx_ref.dtype)
    o = IDX_DIM
    w_ref[...] = x[:, o:o + IDX_HEADS]
    o += IDX_HEADS
    beta_ref[...] = _sigmoid(x[:, o:o + DELTA_HEADS])
    o += DELTA_HEADS
    g_ref[...] = -jnp.exp(alog_ref[...]) * _softplus(x[:, o:o + DELTA_HEADS] + dt_ref[...])


def _cd_small(small, norm_kidx, a_log, dt_bias, tm=1024):
    t, c = small.shape
    row = lambda n: pl.BlockSpec((1, n), lambda i: (0, 0))
    out = lambda n: pl.BlockSpec((tm, n), lambda i: (i, 0))
    return pl.pallas_call(
        _cd_small_kernel,
        grid=(t // tm,),
        in_specs=[pl.BlockSpec((tm, c), lambda i: (i, 0)), row(IDX_DIM), row(DELTA_HEADS), row(DELTA_HEADS)],
        out_specs=[out(IDX_DIM), out(IDX_HEADS), out(DELTA_HEADS), out(DELTA_HEADS)],
        out_shape=[jax.ShapeDtypeStruct((t, IDX_DIM), BF16), jax.ShapeDtypeStruct((t, IDX_HEADS), F32),
                   jax.ShapeDtypeStruct((t, DELTA_HEADS), F32), jax.ShapeDtypeStruct((t, DELTA_HEADS), F32)],
        compiler_params=_cparams("parallel"),
        name="cd_small_prep",
    )(small, norm_kidx.reshape(1, -1).astype(F32), a_log.reshape(1, -1).astype(F32),
      dt_bias.reshape(1, -1).astype(F32))


def _gdn_conv_kernel(x_ref, halo_ref, w_ref, o_ref, *, tiles_per_seq):
    tm = x_ref.shape[0]
    first = pl.program_id(0) % tiles_per_seq == 0
    ext = jnp.concatenate([jnp.where(first, 0.0, halo_ref[...]), x_ref[...]], axis=0)
    w = w_ref[...]
    y = ext * w[DELTA_CONV - 1:DELTA_CONV, :]
    for back in range(1, DELTA_CONV):
        y = y + pltpu.roll(ext, back, 0) * w[DELTA_CONV - 1 - back:DELTA_CONV - back, :]
    y = y[CONV_HALO:, :]
    y = y * _sigmoid(y)
    for hd in range(3 * DELTA_HEADS):
        cols = slice(hd * HEAD_DIM, (hd + 1) * HEAD_DIM)
        t = y[:, cols]
        if hd < 2 * DELTA_HEADS:
            t = t * lax.rsqrt(jnp.sum(t * t, axis=-1, keepdims=True) + L2_EPS)
        if hd < DELTA_HEADS:
            t = t * (HEAD_DIM ** -0.5)
        o_ref[:, cols] = t


def _gdn_conv(x, conv_w, seq, tm=256):
    t = x.shape[0]
    c = conv_w.shape[1]
    halo_blocks_per_tile = tm // CONV_HALO
    return pl.pallas_call(
        functools.partial(_gdn_conv_kernel, tiles_per_seq=seq // tm),
        grid=(t // tm,),
        in_specs=[
            pl.BlockSpec((tm, c), lambda i: (i, 0)),
            pl.BlockSpec((CONV_HALO, c), lambda i: (jnp.maximum(i * halo_blocks_per_tile - 1, 0), 0)),
            pl.BlockSpec((DELTA_CONV, c), lambda i: (0, 0)),
        ],
        out_specs=pl.BlockSpec((tm, c), lambda i: (i, 0)),
        out_shape=jax.ShapeDtypeStruct((t, c), F32),
        compiler_params=_cparams("parallel"),
        name="gdn_conv",
    )(x, x, conv_w.astype(F32))


def _exact_nt(a, b):
    return lax.dot_general(a, b, (((1,), (1,)), ((), ())), preferred_element_type=F32,
                           precision=lax.Precision.HIGHEST)


def _bdot(a, b):
    return jnp.dot(a.astype(BF16), b.astype(BF16), preferred_element_type=F32)


def _bdot_nt(a, b):
    return _nt_dot(a.astype(BF16), b.astype(BF16))


def _bdot_tn(a, b):
    return lax.dot_general(a.astype(BF16), b.astype(BF16), (((0,), (0,)), ((), ())), preferred_element_type=F32)


def _unit_lower_inverses(mats, eye):
    invs = [eye - a for a in mats]
    powers = list(mats)
    span = 2
    while span < mats[0].shape[0]:
        powers = [_bdot(p, p) for p in powers]
        invs = [inv + _bdot(inv, p) for inv, p in zip(invs, powers)]
        span *= 2
    return invs


def _gdn_kernel(q_ref, k_ref, v_ref, z_ref, g_ref, beta_ref, gain_ref, o_ref, state_ref):
    c = DELTA_CHUNK

    @pl.when(pl.program_id(1) == 0)
    def _():
        state_ref[...] = jnp.zeros_like(state_ref)

    row = lax.broadcasted_iota(jnp.int32, (c, c), 0)
    col = lax.broadcasted_iota(jnp.int32, (c, c), 1)
    lower = col <= row
    lower_f = jnp.where(lower, 1.0, 0.0)
    eye = jnp.where(col == row, 1.0, 0.0)
    gain = gain_ref[...]
    heads = range(DELTA_HEADS)
    units = [(n, hd) for n in range(GDN_CHUNKS_PER_STEP) for hd in heads]
    block = lambda ref, n, hd: ref[n * c:(n + 1) * c, hd * HEAD_DIM:(hd + 1) * HEAD_DIM]

    gc_rows = [_exact_nt(g_ref[n], lower_f) for n in range(GDN_CHUNKS_PER_STEP)]
    gc_cols = [_exact_nt(lower_f, g_ref[n]) for n in range(GDN_CHUNKS_PER_STEP)]
    beta_cols = [_exact_nt(eye, beta_ref[n]) for n in range(GDN_CHUNKS_PER_STEP)]
    gc_col = {(n, hd): gc_cols[n][:, hd:hd + 1] for n, hd in units}
    beta_col = {(n, hd): beta_cols[n][:, hd:hd + 1] for n, hd in units}
    decay = {u: jnp.exp(jnp.where(lower, gc_col[u] - gc_rows[u[0]][u[1]:u[1] + 1, :], NEG_INF)) for u in units}
    kb = {u: block(k_ref, *u) * beta_col[u] for u in units}
    kk = {u: _bdot_nt(kb[u], block(k_ref, *u)) for u in units}
    qk = {u: _bdot_nt(block(q_ref, *u), block(k_ref, *u)) for u in units}
    invs = _unit_lower_inverses([jnp.where(col < row, kk[u] * decay[u], 0.0) for u in units], eye)
    sol = {u: _bdot(inv, jnp.concatenate([block(v_ref, *u) * beta_col[u], kb[u] * jnp.exp(gc_col[u])], axis=1))
           for u, inv in zip(units, invs)}

    states = [state_ref[hd] for hd in heads]
    for n in range(GDN_CHUNKS_PER_STEP):
        g_last = {hd: gc_col[(n, hd)][c - 1:c, :] for hd in heads}
        both = [_bdot(jnp.concatenate([sol[(n, hd)][:, HEAD_DIM:], block(q_ref, n, hd) * jnp.exp(gc_col[(n, hd)])],
                                      axis=0), states[hd]) for hd in heads]
        v_new = [sol[(n, hd)][:, :HEAD_DIM] - both[hd][:c] for hd in heads]
        intra = [_bdot(qk[(n, hd)] * decay[(n, hd)], v_new[hd]) for hd in heads]
        grow = [_bdot_tn(block(k_ref, n, hd) * jnp.exp(g_last[hd] - gc_col[(n, hd)]), v_new[hd]) for hd in heads]
        for hd in heads:
            states[hd] = states[hd] * jnp.exp(g_last[hd]) + grow[hd]
            o = both[hd][c:] + intra[hd]
            z = block(z_ref, n, hd)
            normed = o * lax.rsqrt(jnp.mean(o * o, axis=-1, keepdims=True) + RMS_EPS) * gain
            o_ref[n * c:(n + 1) * c, hd * HEAD_DIM:(hd + 1) * HEAD_DIM] = (normed * (z * _sigmoid(z))).astype(o_ref.dtype)
    for hd in heads:
        state_ref[hd] = states[hd]


def _gated_delta(qkv, z_src, z_block, g, beta, norm_out, batch, seq):
    c = DELTA_CHUNK
    tt = GDN_CHUNKS_PER_STEP * c
    steps = seq // tt
    t = batch * seq
    by_chunk = lambda a: a.reshape(t // c, c, DELTA_HEADS).transpose(0, 2, 1)
    tok = lambda src_block: pl.BlockSpec((tt, DELTA_WIDTH), lambda b, i: (b * steps + i, src_block))
    chunk_rows = pl.BlockSpec((GDN_CHUNKS_PER_STEP, DELTA_HEADS, c), lambda b, i: (b * steps + i, 0, 0))
    return pl.pallas_call(
        _gdn_kernel,
        grid=(batch, steps),
        in_specs=[tok(0), tok(1), tok(2), tok(z_block), chunk_rows, chunk_rows,
                  pl.BlockSpec((1, HEAD_DIM), lambda b, i: (0, 0))],
        out_specs=pl.BlockSpec((tt, DELTA_WIDTH), lambda b, i: (b * steps + i, 0)),
        out_shape=jax.ShapeDtypeStruct((t, DELTA_WIDTH), BF16),
        scratch_shapes=[pltpu.VMEM((DELTA_HEADS, HEAD_DIM, HEAD_DIM), F32)],
        compiler_params=_cparams("parallel", "arbitrary"),
        name="gated_delta",
    )(qkv, qkv, qkv, z_src, by_chunk(g), by_chunk(beta), norm_out.reshape(1, HEAD_DIM).astype(F32))


SB_WIDTH = 1024
DSA_Q_RANK = 256
DSA_WIDTH = DSA_HEADS * HEAD_DIM


def _stick_pool_mixer(h, norm_g, w_in, pool_w, pool_scale, w_out, batch, seq):
    qkv = _linear([h], [w_in[:, :3 * SB_WIDTH]], norm_g=norm_g, out_dtype=BF16, tn=1024, name="in_ab_qkv")
    u = _linear([h], [w_in[:, 3 * SB_WIDTH:]], norm_g=norm_g, name="in_ab_pool")
    o_a = _sb_attention(qkv, batch, seq, SB_WIDTH // HEAD_DIM)
    o_b = _multiscale_pool(u, pool_w, pool_scale, seq)
    return _linear([o_a, o_b], [w_out[:SB_WIDTH], w_out[SB_WIDTH:]], residual=h, tn=1024, name="out_ab")


def _dsa_delta_mixer(h, norm_g, w_in, w_uq, w_iq, norm_cq, norm_kidx, conv_w, a_log, dt_bias, norm_out, w_out,
                     rel_bias, batch, seq):
    sizes = [DSA_Q_RANK, DSA_WIDTH, DSA_WIDTH, IDX_DIM, IDX_HEADS, 3 * DELTA_WIDTH, DELTA_HEADS, DELTA_HEADS,
             DELTA_WIDTH]
    offs = [0]
    for n in sizes:
        offs.append(offs[-1] + n)
    col = lambda a, b_: w_in[:, offs[a]:offs[b_]]
    pad = jnp.zeros((w_in.shape[0], 128 - (IDX_DIM + IDX_HEADS + 2 * DELTA_HEADS)), w_in.dtype)
    w_f32 = jnp.concatenate([col(5, 6), col(8, 9), col(0, 1), col(3, 5), col(6, 8), pad], axis=1)
    kv = _linear([h], [col(1, 3)], norm_g=norm_g, out_dtype=BF16, tn=1024, name="in_cd_kv")
    cd = _linear([h], [w_f32], norm_g=norm_g, tn=896, name="in_cd_rest")
    z_block = 3
    c_q = cd[:, 4 * DELTA_WIDTH:4 * DELTA_WIDTH + DSA_Q_RANK]
    small = cd[:, 4 * DELTA_WIDTH + DSA_Q_RANK:]
    qq = _linear([c_q], [jnp.concatenate([w_uq, w_iq], axis=1)], norm_g=norm_cq, out_dtype=BF16, name="dsa_queries")
    kidx, w_idx, beta, g = _cd_small(small, norm_kidx, a_log, dt_bias)
    mask = _dsa_select(qq, w_idx.T, kidx, batch, seq)
    o_c = _dsa_attention(qq, kv, mask, rel_bias, batch, seq)
    conv = _gdn_conv(cd, conv_w, seq)
    o_d = _gated_delta(conv, cd, z_block, g, beta, norm_out, batch, seq)
    return _linear([o_c, o_d], [w_out[:DSA_WIDTH], w_out[DSA_WIDTH:]], residual=h, tn=1024, name="out_cd")


def kernel(x, mem, norm_mix, norm_cross, norm_mem, norm_ffn, norm_final, w_in_ab, pool_w, pool_scale, w_out_ab, w_in_cd, w_uq, w_iq, norm_cq, norm_kidx, conv_w, a_log, dt_bias, norm_delta_out, w_out_cd, rel_bias, xattn_wq, xattn_wkv, xattn_wo, peer_wq, peer_subkeys, peer_u, peer_v):
    b, s, d = x.shape
    n_mem = mem.shape[1]
    depth = norm_mix.shape[0]
    h = x.reshape(b * s, d)
    mem2 = mem.reshape(b * n_mem, d)
    for layer in range(depth):
        j = layer // 2
        if layer % 2 == 0:
            h = _stick_pool_mixer(h, norm_mix[layer], w_in_ab[j], pool_w[j], pool_scale[j], w_out_ab[j], b, s)
        else:
            h = _dsa_delta_mixer(h, norm_mix[layer], w_in_cd[j], w_uq[j], w_iq[j], norm_cq[j], norm_kidx[j],
                                 conv_w[j], a_log[j], dt_bias[j], norm_delta_out[j], w_out_cd[j], rel_bias, b, s)
        kv = _linear([mem2], [xattn_wkv[layer]], norm_g=norm_mem[layer], out_dtype=BF16, tm=n_mem, name="xattn_kv")
        h = _cross_attention(h, norm_cross[layer], xattn_wq[layer], kv, xattn_wo[layer], s, n_mem)
        h = _peer_ffn(h, norm_ffn[layer], peer_wq[layer], peer_subkeys[layer], peer_u[layer], peer_v[layer])
    return _rmsnorm(h, norm_final).reshape(b, s, d)
```

```python
import functools
import math

import jax
import jax.numpy as jnp
from jax import lax
from jax.experimental import pallas as pl
from jax.experimental.pallas import tpu as pltpu

F32 = jnp.float32
BF16 = jnp.bfloat16

HEAD_DIM = 128
RMS_EPS = 1e-6
EXP_UNDERFLOW = -104.0
VMEM_LIMIT_BYTES = 56 * 1024 * 1024


def _cparams(*semantics):
    return pltpu.CompilerParams(dimension_semantics=semantics, vmem_limit_bytes=VMEM_LIMIT_BYTES)


def _nt_dot(a, b):
    return lax.dot_general(a, b, (((1,), (1,)), ((), ())), preferred_element_type=F32)


def _dot(a, b):
    return jnp.dot(a, b, preferred_element_type=F32)


def _rms_norm_rows(xf, g):
    return xf * lax.rsqrt(jnp.mean(xf * xf, axis=-1, keepdims=True) + RMS_EPS) * g


def _linear_kernel(*refs, n_lhs, has_norm, has_res):
    pos = 0
    x_refs = refs[pos:pos + n_lhs]; pos += n_lhs
    w_refs = refs[pos:pos + n_lhs]; pos += n_lhs
    g_ref = None
    if has_norm:
        g_ref = refs[pos]; pos += 1
    r_ref = None
    if has_res:
        r_ref = refs[pos]; pos += 1
    o_ref = refs[pos]; pos += 1
    xn_ref = refs[pos] if has_norm else None

    if has_norm:
        @pl.when(pl.program_id(1) == 0)
        def _():
            xn_ref[...] = _rms_norm_rows(x_refs[0][...].astype(F32), g_ref[...]).astype(BF16)
        acc = _dot(xn_ref[...], w_refs[0][...])
    else:
        acc = _dot(x_refs[0][...].astype(BF16), w_refs[0][...])
        for x_ref, w_ref in zip(x_refs[1:], w_refs[1:]):
            acc = acc + _dot(x_ref[...].astype(BF16), w_ref[...])
    if has_res:
        acc = acc + r_ref[...]
    o_ref[...] = acc.astype(o_ref.dtype)


def _linear(xs, ws, *, norm_g=None, residual=None, out_dtype=F32, tm=512, tn=None, name="linear"):
    xs = list(xs)
    ws = [w.astype(BF16) for w in ws]
    t = xs[0].shape[0]
    n = ws[0].shape[1]
    if tn is None:
        tn = n
    assert t % tm == 0 and n % tn == 0
    has_norm = norm_g is not None
    has_res = residual is not None
    assert not has_norm or len(xs) == 1
    in_specs = [pl.BlockSpec((tm, x.shape[1]), lambda i, j: (i, 0)) for x in xs]
    in_specs += [pl.BlockSpec((w.shape[0], tn), lambda i, j: (0, j)) for w in ws]
    args = xs + ws
    if has_norm:
        in_specs.append(pl.BlockSpec((1, xs[0].shape[1]), lambda i, j: (0, 0)))
        args.append(norm_g.reshape(1, -1).astype(F32))
    if has_res:
        in_specs.append(pl.BlockSpec((tm, tn), lambda i, j: (i, j)))
        args.append(residual)
    scratch = [pltpu.VMEM((tm, xs[0].shape[1]), BF16)] if has_norm else []
    return pl.pallas_call(
        functools.partial(_linear_kernel, n_lhs=len(xs), has_norm=has_norm, has_res=has_res),
        grid=(t // tm, n // tn),
        in_specs=in_specs,
        out_specs=pl.BlockSpec((tm, tn), lambda i, j: (i, j)),
        out_shape=jax.ShapeDtypeStruct((t, n), out_dtype),
        scratch_shapes=scratch,
        compiler_params=_cparams("parallel", "arbitrary"),
        name=name,
    )(*args)


SB_BLOCK = 256


SB_HEADS_PER_STEP = 2


def _sb_attn_kernel(q_ref, k_ref, v_ref, o_ref):
    tq = q_ref.shape[0]
    qi = pl.program_id(2)
    scale = HEAD_DIM ** -0.5
    heads = range(SB_HEADS_PER_STEP)
    head_cols = [slice(hd * HEAD_DIM, (hd + 1) * HEAD_DIM) for hd in heads]
    row = lax.broadcasted_iota(jnp.int32, (tq, tq), 0)
    col = lax.broadcasted_iota(jnp.int32, (tq, tq), 1)
    suffix = jnp.where(row > col, 1.0, 0.0).astype(BF16)
    before = col < row

    def block(j, carries, accs, diagonal):
        start = pl.multiple_of(j * tq, tq)
        z = [_nt_dot(q_ref[:, cols], k_ref[pl.ds(start, tq), cols]) * scale for cols in head_cols]
        lk = [-(jnp.maximum(zh, 0.0) + jnp.log1p(jnp.exp(-jnp.abs(zh)))) for zh in z]
        if diagonal:
            lk = [jnp.where(before, t, 0.0) for t in lk]
        lk_hi = [t.astype(BF16) for t in lk]
        lk_lo = [(t - hi.astype(F32)).astype(BF16) for t, hi in zip(lk, lk_hi)]
        right = [_dot(hi, suffix) + _dot(lo, suffix) for hi, lo in zip(lk_hi, lk_lo)]
        w = [jnp.exp(z[hd] + lk[hd] + right[hd] + carries[hd]) for hd in heads]
        if diagonal:
            w = [jnp.where(before, t, 0.0) for t in w]
        accs = tuple(accs[hd] + _dot(w[hd].astype(BF16), v_ref[pl.ds(start, tq), head_cols[hd]]) for hd in heads)
        carries = tuple(carries[hd] + right[hd][:, 0:1] + lk[hd][:, 0:1] for hd in heads)
        return carries, accs

    zeros = lambda width: tuple(jnp.zeros((tq, width), F32) for _ in heads)
    carries, accs = block(qi, zeros(1), zeros(HEAD_DIM), True)

    def cond(state):
        j, carries, _ = state
        live = carries[0]
        for c in carries[1:]:
            live = jnp.maximum(live, c)
        return jnp.logical_and(j >= 0, jnp.max(live) > EXP_UNDERFLOW)

    def body(state):
        j, carries, accs = state
        carries, accs = block(j, carries, accs, False)
        return j - 1, carries, accs

    _, _, accs = lax.while_loop(cond, body, (qi - 1, carries, accs))
    for hd in heads:
        o_ref[:, head_cols[hd]] = accs[hd].astype(o_ref.dtype)


def _sb_attention(qkv, batch, seq, heads):
    tq = SB_BLOCK
    nq = seq // tq
    groups = heads // SB_HEADS_PER_STEP
    width = SB_HEADS_PER_STEP * HEAD_DIM
    resident = lambda index_map: pl.BlockSpec((seq, width), index_map, pipeline_mode=pl.Buffered(1))
    return pl.pallas_call(
        _sb_attn_kernel,
        grid=(batch, groups, nq),
        in_specs=[
            pl.BlockSpec((tq, width), lambda b, h, i: (b * nq + i, h)),
            resident(lambda b, h, i: (b, groups + h)),
            resident(lambda b, h, i: (b, 2 * groups + h)),
        ],
        out_specs=pl.BlockSpec((tq, width), lambda b, h, i: (b * nq + i, h)),
        out_shape=jax.ShapeDtypeStruct((batch * seq, heads * HEAD_DIM), BF16),
        compiler_params=_cparams("parallel", "parallel", "arbitrary"),
        name="sb_attention",
    )(qkv, qkv, qkv)


POOL_WINDOWS = (2, 4, 8, 16)
POOL_DIM = 256
POOL_HALO = 16


def _pool_kernel(u_ref, halo_ref, w_ref, scale_ref, o_ref, *, tiles_per_seq):
    tm = u_ref.shape[0]
    tile_in_seq = pl.program_id(0) % tiles_per_seq
    u = u_ref[...]
    halo = jnp.where(tile_in_seq == 0, 0.0, halo_ref[...])
    ext = jnp.concatenate([halo, u], axis=0)
    pos = tile_in_seq * tm + lax.broadcasted_iota(jnp.int32, (tm, 1), 0)
    for g, win in enumerate(POOL_WINDOWS):
        cols = slice(g * POOL_DIM, (g + 1) * POOL_DIM)
        a = ext[:, cols]
        k = 1
        while k < win:
            a = a + pltpu.roll(a, k, 0)
            k *= 2
        count = jnp.minimum(pos + 1, win).astype(F32)
        d = a[POOL_HALO:, :] / count - u[:, cols]
        y = _dot(d.astype(BF16), w_ref[g]) * scale_ref[:, cols]
        o_ref[:, cols] = y.astype(o_ref.dtype)


def _multiscale_pool(u, pool_w, pool_scale, seq, tm=512):
    t, c = u.shape
    assert seq % tm == 0 and tm % POOL_HALO == 0 and max(POOL_WINDOWS) <= POOL_HALO
    halo_blocks_per_tile = tm // POOL_HALO
    return pl.pallas_call(
        functools.partial(_pool_kernel, tiles_per_seq=seq // tm),
        grid=(t // tm,),
        in_specs=[
            pl.BlockSpec((tm, c), lambda i: (i, 0)),
            pl.BlockSpec((POOL_HALO, c), lambda i: (jnp.maximum(i * halo_blocks_per_tile - 1, 0), 0)),
            pl.BlockSpec(pool_w.shape, lambda i: (0, 0, 0)),
            pl.BlockSpec((1, c), lambda i: (0, 0)),
        ],
        out_specs=pl.BlockSpec((tm, c), lambda i: (i, 0)),
        out_shape=jax.ShapeDtypeStruct((t, c), BF16),
        compiler_params=_cparams("parallel"),
        name="multiscale_pool",
    )(u, u, pool_w.astype(BF16), pool_scale.reshape(1, c).astype(F32))


XATTN_HEADS = 4


def _xattn_kernel(h_ref, g_ref, wq_ref, k_ref, v_ref, wo_ref, o_ref):
    scale = HEAD_DIM ** -0.5
    h = h_ref[...]
    hn = _rms_norm_rows(h, g_ref[...]).astype(BF16)
    q = _dot(hn, wq_ref[...]).astype(BF16)
    outs = []
    for hd in range(XATTN_HEADS):
        cols = slice(hd * HEAD_DIM, (hd + 1) * HEAD_DIM)
        logits = _nt_dot(q[:, cols], k_ref[:, cols]) * scale
        logits = logits - jnp.max(logits, axis=-1, keepdims=True)
        e = jnp.exp(logits)
        p = e / jnp.sum(e, axis=-1, keepdims=True)
        outs.append(_dot(p.astype(BF16), v_ref[:, cols]))
    o = jnp.concatenate(outs, axis=-1).astype(BF16)
    o_ref[...] = h + _dot(o, wo_ref[...])


def _cross_attention(h, norm_g, wq, kv, wo, seq, mem_tokens, tm=512):
    t, d = h.shape
    hw = XATTN_HEADS * HEAD_DIM
    tiles_per_seq = seq // tm
    return pl.pallas_call(
        _xattn_kernel,
        grid=(t // tm,),
        in_specs=[
            pl.BlockSpec((tm, d), lambda i: (i, 0)),
            pl.BlockSpec((1, d), lambda i: (0, 0)),
            pl.BlockSpec((d, hw), lambda i: (0, 0)),
            pl.BlockSpec((mem_tokens, hw), lambda i: (i // tiles_per_seq, 0)),
            pl.BlockSpec((mem_tokens, hw), lambda i: (i // tiles_per_seq, 1)),
            pl.BlockSpec((hw, d), lambda i: (0, 0)),
        ],
        out_specs=pl.BlockSpec((tm, d), lambda i: (i, 0)),
        out_shape=jax.ShapeDtypeStruct((t, d), F32),
        compiler_params=_cparams("parallel"),
        name="cross_attention",
    )(h, norm_g.reshape(1, d).astype(F32), wq.astype(BF16), kv, kv, wo.astype(BF16))


def _rmsnorm_kernel(x_ref, g_ref, o_ref):
    o_ref[...] = _rms_norm_rows(x_ref[...], g_ref[...])


def _rmsnorm(x, g, tm=512):
    t, d = x.shape
    return pl.pallas_call(
        _rmsnorm_kernel,
        grid=(t // tm,),
        in_specs=[pl.BlockSpec((tm, d), lambda i: (i, 0)), pl.BlockSpec((1, d), lambda i: (0, 0))],
        out_specs=pl.BlockSpec((tm, d), lambda i: (i, 0)),
        out_shape=jax.ShapeDtypeStruct((t, d), F32),
        compiler_params=_cparams("parallel"),
        name="final_rmsnorm",
    )(x, g.reshape(1, d).astype(F32))


PEER_HEADS = 8
PEER_KEYS = 128
PEER_TOPK = 16
NEG_INF = float("-inf")


PEER_PAIR_BLOCKS = ((0, 16), (16, 8), (24, 8), (32, 8), (40, 8), (48, 8), (56, 8), (64, 8))
PEER_PAIR_TAIL = 72
PEER_PAIR_ROWS = PEER_PAIR_TAIL + 8


def _top16_rows(s, vals_ref, idx_ref=None):
    rows = lax.broadcasted_iota(jnp.int32, s.shape, 0)
    work = s
    for r in range(PEER_TOPK):
        m = jnp.max(work, axis=0, keepdims=True)
        first = jnp.min(jnp.where(work == m, rows, PEER_KEYS), axis=0, keepdims=True)
        vals_ref[r:r + 1, :] = m
        if idx_ref is not None:
            idx_ref[r:r + 1, :] = first
        work = jnp.where(rows == first, NEG_INF, work)
    return work == NEG_INF


def _peer_route_kernel(q_ref, keys_ref, thr_ref, c1_ref, s2_ref, e2_ref, a_ref, b_ref, cand_ref, idx_ref):
    rows = lax.broadcasted_iota(jnp.int32, (PEER_KEYS, q_ref.shape[0]), 0)
    for hd in range(PEER_HEADS):
        s1 = _nt_dot(keys_ref[2 * hd], q_ref[:, (2 * hd) * PEER_KEYS:(2 * hd + 1) * PEER_KEYS])
        s2 = _nt_dot(keys_ref[2 * hd + 1], q_ref[:, (2 * hd + 1) * PEER_KEYS:(2 * hd + 2) * PEER_KEYS])
        top1 = _top16_rows(s1, a_ref, idx_ref)
        top2 = _top16_rows(s2, b_ref)
        a = a_ref[...]
        b = b_ref[...]
        rank_in_block = lax.broadcasted_iota(jnp.int32, (8, a.shape[1]), 0)
        for i, (off, n_rows) in enumerate(PEER_PAIR_BLOCKS):
            pair = a_ref[i:i + 1, :] + b_ref[0:n_rows, :]
            feasible = PEER_TOPK // (i + 1)
            if feasible < n_rows:
                pair = jnp.where(rank_in_block < feasible, pair, NEG_INF)
            cand_ref[off:off + n_rows, :] = pair
        cand_ref[PEER_PAIR_TAIL:PEER_PAIR_TAIL + 8, :] = a_ref[8:16, :] + b_ref[0:1, :]
        work = cand_ref[...]
        best = a[0:1, :] + b[0:1, :]
        taken = jnp.zeros_like(best)
        tau = best
        z = jnp.zeros_like(best)
        for _ in range(PEER_TOPK):
            m = jnp.max(work, axis=0, keepdims=True)
            eq = work == m
            cnt = jnp.sum(jnp.where(eq, 1.0, 0.0), axis=0, keepdims=True)
            room = PEER_TOPK - taken
            use = jnp.clip(jnp.minimum(cnt, room), 0.0, None)
            z = z + use * jnp.exp(m - best)
            tau = jnp.where(room > 0.0, m, tau)
            taken = taken + cnt
            work = jnp.where(eq, NEG_INF, work)
        idx = idx_ref[...]
        thr = jnp.full(s1.shape, jnp.inf, F32)
        for r in range(PEER_TOPK):
            if r < len(PEER_PAIR_BLOCKS):
                off, n_rows = PEER_PAIR_BLOCKS[r]
                kept_partner = jnp.where(cand_ref[off:off + n_rows, :] >= tau, b_ref[0:n_rows, :], jnp.inf)
                thr_r = jnp.min(kept_partner, axis=0, keepdims=True)
            else:
                row = PEER_PAIR_TAIL + r - len(PEER_PAIR_BLOCKS)
                thr_r = jnp.where(cand_ref[row:row + 1, :] >= tau, b[0:1, :], jnp.inf)
            thr = jnp.where(rows == idx[r:r + 1, :], thr_r, thr)
        thr_ref[hd] = thr
        c1_ref[hd] = jnp.where(top1, jnp.exp(s1 - a[0:1, :]), 0.0) / z
        s2_ref[hd] = jnp.where(top2, s2, NEG_INF)
        e2_ref[hd] = jnp.exp(s2 - b[0:1, :])


def _peer_route(q, subkeys, tm=256):
    t = q.shape[0]
    keys = subkeys.reshape(PEER_HEADS * 2, PEER_KEYS, PEER_KEYS).astype(BF16)
    big = jax.ShapeDtypeStruct((PEER_HEADS, PEER_KEYS, t), F32)
    big_spec = pl.BlockSpec((PEER_HEADS, PEER_KEYS, tm), lambda i: (0, 0, i))
    return pl.pallas_call(
        _peer_route_kernel,
        grid=(t // tm,),
        in_specs=[
            pl.BlockSpec((tm, q.shape[1]), lambda i: (i, 0)),
            pl.BlockSpec(keys.shape, lambda i: (0, 0, 0)),
        ],
        out_specs=[big_spec, big_spec, big_spec, big_spec],
        out_shape=[big, big, big, big],
        scratch_shapes=[
            pltpu.VMEM((PEER_TOPK, tm), F32),
            pltpu.VMEM((PEER_TOPK, tm), F32),
            pltpu.VMEM((PEER_PAIR_ROWS, tm), F32),
            pltpu.VMEM((PEER_TOPK, tm), jnp.int32),
        ],
        compiler_params=_cparams("parallel"),
        name="peer_route",
    )(q, keys)


PEER_I1_PER_TILE = 8


def _peer_expert_kernel(h_ref, g_ref, u_ref, v_ref, thr_ref, c1_ref, s2_ref, e2_ref, o_ref,
                        xn_ref, acc_ref, act_even_ref, act_odd_ref, *, n_tiles):
    e = pl.program_id(1)
    act_refs = (act_even_ref, act_odd_ref)

    def up_project(slot):
        act_refs[slot][...] = _nt_dot(u_ref[...], xn_ref[...])

    def finish(slot):
        act = act_refs[slot][...]
        act = 0.5 * act * (1.0 + lax.erf(act * (2.0 ** -0.5)))
        gates = []
        for i1 in range(PEER_I1_PER_TILE):
            gate = None
            for hd in range(PEER_HEADS):
                kept = jnp.where(s2_ref[hd] >= thr_ref[hd, i1:i1 + 1, :], e2_ref[hd], 0.0)
                term = c1_ref[hd, i1:i1 + 1, :] * kept
                gate = term if gate is None else gate + term
            gates.append(gate)
        weighted = (jnp.concatenate(gates, axis=0) * act).astype(BF16)
        acc_ref[...] += lax.dot_general(weighted, v_ref[...], (((0,), (0,)), ((), ())),
                                        preferred_element_type=F32)

    @pl.when(e == 0)
    def _():
        xn_ref[...] = _rms_norm_rows(h_ref[...], g_ref[...]).astype(BF16)
        acc_ref[...] = jnp.zeros_like(acc_ref)
        up_project(0)

    middle = jnp.logical_and(e > 0, e < n_tiles)

    @pl.when(jnp.logical_and(middle, e % 2 == 1))
    def _():
        up_project(1)
        finish(0)

    @pl.when(jnp.logical_and(middle, e % 2 == 0))
    def _():
        up_project(0)
        finish(1)

    @pl.when(e == n_tiles)
    def _():
        finish((n_tiles - 1) % 2)
        o_ref[...] = h_ref[...] + acc_ref[...]


def _peer_experts(h, norm_g, u_tab, v_tab, route, tm=512):
    thr, c1, s2, e2 = route
    t, d = h.shape
    n_exp = u_tab.shape[0]
    te = PEER_I1_PER_TILE * PEER_KEYS
    n_tiles = n_exp // te
    prev = lambda e: jnp.maximum(e - 1, 0)
    head_rows = pl.BlockSpec((PEER_HEADS, PEER_I1_PER_TILE, tm), lambda i, e: (0, prev(e), i))
    head_full = pl.BlockSpec((PEER_HEADS, PEER_KEYS, tm), lambda i, e: (0, 0, i))
    return pl.pallas_call(
        functools.partial(_peer_expert_kernel, n_tiles=n_tiles),
        grid=(t // tm, n_tiles + 1),
        in_specs=[
            pl.BlockSpec((tm, d), lambda i, e: (i, 0)),
            pl.BlockSpec((1, d), lambda i, e: (0, 0)),
            pl.BlockSpec((te, d), lambda i, e: (jnp.minimum(e, n_tiles - 1), 0)),
            pl.BlockSpec((te, d), lambda i, e: (prev(e), 0)),
            head_rows, head_rows, head_full, head_full,
        ],
        out_specs=pl.BlockSpec((tm, d), lambda i, e: (i, 0)),
        out_shape=jax.ShapeDtypeStruct((t, d), F32),
        scratch_shapes=[pltpu.VMEM((tm, d), BF16), pltpu.VMEM((tm, d), F32),
                        pltpu.VMEM((te, tm), F32), pltpu.VMEM((te, tm), F32)],
        compiler_params=_cparams("parallel", "arbitrary"),
        name="peer_experts",
    )(h, norm_g.reshape(1, d).astype(F32), u_tab.astype(BF16), v_tab.astype(BF16), thr, c1, s2, e2)


def _peer_ffn(h, norm_g, w_q, subkeys, u_tab, v_tab):
    q = _linear([h], [w_q], norm_g=norm_g, out_dtype=BF16, tn=1024, name="peer_q")
    return _peer_experts(h, norm_g, u_tab, v_tab, _peer_route(q, subkeys))


DSA_HEADS = 8
IDX_HEADS = 16
IDX_DIM = 64
DSA_TOPK = 256
DSA_QUERIES = 256
DSA_SELECT_KEYS = 256
DSA_ATTN_KEYS = 512
DSA_TILE = 128
DSA_HEADS_PER_STEP = 2
MASKED = -1e30
REL_BUCKETS = 32
REL_MAX_DIST = 2048
DSA_BIAS_TILES = REL_MAX_DIST // DSA_TILE + 2
INT32_MIN = -2 ** 31


def _dsa_select_kernel(qi_ref, w_ref, kidx_ref, mask_ref, key_ref, *, topk):
    ck = DSA_SELECT_KEYS
    sub = DSA_TILE
    tq = qi_ref.shape[0]
    n_total = mask_ref.shape[0] // ck
    qb = pl.program_id(1)
    n_chunks = (qb + 1) * (tq // ck)
    qi = qi_ref[...]
    w = w_ref[...] * (IDX_HEADS ** -0.5) * (IDX_DIM ** -0.5)
    qpos = qb * tq + lax.broadcasted_iota(jnp.int32, (1, tq), 1)
    krow_sub = lax.broadcasted_iota(jnp.int32, (sub, 1), 0)
    krow = lax.broadcasted_iota(jnp.int32, (ck, 1), 0)

    def score_chunk(c, carry):
        for part in range(ck // sub):
            start = pl.multiple_of(c * ck + part * sub, sub)
            kc = kidx_ref[pl.ds(start, sub), :]
            sc = jnp.zeros((sub, tq), F32)
            for hd in range(IDX_HEADS):
                rel = jnp.maximum(_nt_dot(kc, qi[:, hd * IDX_DIM:(hd + 1) * IDX_DIM]), 0.0)
                sc = sc + w[hd:hd + 1, :] * rel
            sc = jnp.where(start + krow_sub <= qpos, sc + 0.0, NEG_INF)
            bits = pltpu.bitcast(sc, jnp.int32)
            key_ref[pl.ds(start, sub), :] = jnp.where(bits < 0, bits ^ jnp.int32(0x7FFFFFFF), bits)
        return carry

    lax.fori_loop(0, n_chunks, score_chunk, 0)

    @pl.when(n_chunks % 2 == 1)
    def _():
        key_ref[pl.ds(pl.multiple_of(n_chunks * ck, ck), ck), :] = jnp.full((ck, tq), INT32_MIN, jnp.int32)

    def count_ge(cand):
        def chunk_pair(c, acc):
            start = pl.multiple_of(c * (2 * ck), 2 * ck)
            hit = jnp.where(key_ref[pl.ds(start, 2 * ck), :] >= cand, 1, 0)
            return acc + jnp.sum(hit.reshape(2 * ck // 8, 8, tq), axis=0)
        n_pairs = lax.shift_right_logical(n_chunks + 1, 1)
        acc = lax.fori_loop(0, n_pairs, chunk_pair, jnp.zeros((8, tq), jnp.int32))
        return jnp.sum(acc, axis=0, keepdims=True)

    def unresolved(state):
        bit, _, held = state
        return jnp.logical_and(bit < 32, jnp.max(held.astype(F32)) > topk)

    def refine(state):
        bit, kth, held = state
        cand = kth + lax.shift_left(jnp.int32(1), 31 - bit)
        cnt = count_ge(cand)
        ok = cnt >= topk
        return bit + 1, jnp.where(ok, cand, kth), jnp.where(ok, cnt, held)

    _, kth, _ = lax.while_loop(
        unresolved, refine,
        (jnp.int32(0), jnp.full((1, tq), INT32_MIN, jnp.int32), jnp.full((1, tq), n_chunks * ck, jnp.int32)))

    def write_mask(c, carry):
        start = pl.multiple_of(c * ck, ck)
        keep = jnp.logical_and(key_ref[pl.ds(start, ck), :] >= kth, start + krow <= qpos)
        mask_ref[pl.ds(start, ck), :] = jnp.where(keep, 0.0, MASKED).astype(mask_ref.dtype)
        return carry

    lax.fori_loop(0, n_chunks, write_mask, 0)

    def write_rest(c, carry):
        start = pl.multiple_of(c * ck, ck)
        mask_ref[pl.ds(start, ck), :] = jnp.full((ck, tq), MASKED, mask_ref.dtype)
        return carry

    lax.fori_loop(n_chunks, n_total, write_rest, 0)


def _dsa_select(qq, w_t, kidx, batch, seq):
    tq = DSA_QUERIES
    assert seq % (2 * DSA_SELECT_KEYS) == 0 and tq == DSA_SELECT_KEYS and DSA_SELECT_KEYS % DSA_TILE == 0
    nq = seq // tq
    topk = min(DSA_TOPK, seq // 4)
    iw = IDX_HEADS * IDX_DIM
    return pl.pallas_call(
        functools.partial(_dsa_select_kernel, topk=topk),
        grid=(batch, nq),
        in_specs=[
            pl.BlockSpec((tq, iw), lambda b, i: (b * nq + i, 1)),
            pl.BlockSpec((IDX_HEADS, tq), lambda b, i: (0, b * nq + i)),
            pl.BlockSpec((seq, IDX_DIM), lambda b, i: (b, 0)),
        ],
        out_specs=pl.BlockSpec((None, seq, tq), lambda b, i: (b, 0, i)),
        out_shape=jax.ShapeDtypeStruct((batch, seq, seq), BF16),
        scratch_shapes=[pltpu.VMEM((seq, tq), jnp.int32)],
        compiler_params=_cparams("parallel", "arbitrary"),
        name="dsa_select",
    )(qq, w_t, kidx)


def _dsa_attn_kernel(q_ref, k_ref, v_ref, mask_ref, bias_ref, o_ref):
    ck = DSA_ATTN_KEYS
    tile = DSA_TILE
    tq = q_ref.shape[0]
    i = pl.program_id(2)
    scale = HEAD_DIM ** -0.5
    n_steps = lax.div((i + 1) * tq + (ck - 1), ck)

    def step(c, state):
        start = pl.multiple_of(c * ck, ck)
        mask = mask_ref[pl.ds(start, ck), :].astype(F32)
        base = i * (tq // tile) - c * (ck // tile)
        offsets = [[jnp.clip(base + b - a, 0, DSA_BIAS_TILES - 1) for b in range(tq // tile)]
                   for a in range(ck // tile)]
        head_cols = [slice(hd * HEAD_DIM, (hd + 1) * HEAD_DIM) for hd in range(DSA_HEADS_PER_STEP)]
        dots = [_nt_dot(k_ref[pl.ds(start, ck), cols], q_ref[:, cols]) for cols in head_cols]
        new_state = []
        for hd, cols in enumerate(head_cols):
            m, l, acc = state[hd]
            bias = jnp.concatenate(
                [jnp.concatenate([bias_ref[hd, off] for off in row], axis=1) for row in offsets], axis=0)
            s = dots[hd] * scale + bias + mask
            m_new = jnp.maximum(m, jnp.max(s, axis=0, keepdims=True))
            alpha = jnp.exp(m - m_new)
            p = jnp.exp(s - m_new)
            l = l * alpha + jnp.sum(p, axis=0, keepdims=True)
            pv = lax.dot_general(v_ref[pl.ds(start, ck), cols], p.astype(BF16), (((0,), (0,)), ((), ())),
                                 preferred_element_type=F32)
            new_state.append((m_new, l, acc * alpha + pv))
        return tuple(new_state)

    init = (jnp.full((1, tq), MASKED, F32), jnp.zeros((1, tq), F32), jnp.zeros((HEAD_DIM, tq), F32))
    state = lax.fori_loop(0, n_steps, step, (init,) * DSA_HEADS_PER_STEP)
    for hd in range(DSA_HEADS_PER_STEP):
        _, l, acc = state[hd]
        o_ref[:, hd * HEAD_DIM:(hd + 1) * HEAD_DIM] = (acc / l).T.astype(o_ref.dtype)


def _t5_bucket(dist):
    n = jnp.maximum(dist, 0)
    exact = REL_BUCKETS // 2
    nf = jnp.maximum(n, 1).astype(F32)
    log_ratio = jnp.log(nf / exact) / math.log(REL_MAX_DIST / exact)
    large = exact + (log_ratio * (REL_BUCKETS - exact)).astype(jnp.int32)
    return jnp.where(n < exact, n, jnp.minimum(large, REL_BUCKETS - 1))


def _dsa_bias_tiles(rel_bias):
    tile = DSA_TILE
    heads = rel_bias.shape[1]
    offset = jnp.arange(DSA_BIAS_TILES - 1)[:, None, None] * tile
    bucket = _t5_bucket(offset + jnp.arange(tile)[None, None, :] - jnp.arange(tile)[None, :, None])
    near = jnp.zeros((heads,) + bucket.shape, F32)
    for bkt in range(REL_BUCKETS):
        near = jnp.where(bucket[None] == bkt, rel_bias[bkt].astype(F32)[:, None, None, None], near)
    far = jnp.broadcast_to(rel_bias[REL_BUCKETS - 1].astype(F32)[:, None, None, None], (heads, 1, tile, tile))
    return jnp.concatenate([near, far], axis=1)


def _dsa_attention(qq, kv, mask, rel_bias, batch, seq):
    tq = DSA_QUERIES
    assert seq % DSA_ATTN_KEYS == 0 and tq % DSA_TILE == 0 and DSA_ATTN_KEYS % DSA_TILE == 0
    nq = seq // tq
    groups = DSA_HEADS // DSA_HEADS_PER_STEP
    width = DSA_HEADS_PER_STEP * HEAD_DIM
    resident = lambda index_map: pl.BlockSpec((seq, width), index_map, pipeline_mode=pl.Buffered(1))
    return pl.pallas_call(
        _dsa_attn_kernel,
        grid=(batch, groups, nq),
        in_specs=[
            pl.BlockSpec((tq, width), lambda b, h, i: (b * nq + i, h)),
            resident(lambda b, h, i: (b, h)),
            resident(lambda b, h, i: (b, groups + h)),
            pl.BlockSpec((None, seq, tq), lambda b, h, i: (b, 0, i)),
            pl.BlockSpec((DSA_HEADS_PER_STEP, DSA_BIAS_TILES, DSA_TILE, DSA_TILE), lambda b, h, i: (h, 0, 0, 0)),
        ],
        out_specs=pl.BlockSpec((tq, width), lambda b, h, i: (b * nq + i, h)),
        out_shape=jax.ShapeDtypeStruct((batch * seq, DSA_HEADS * HEAD_DIM), BF16),
        compiler_params=_cparams("parallel", "parallel", "arbitrary"),
        name="dsa_attention",
    )(qq, kv, kv, mask, _dsa_bias_tiles(rel_bias))


DELTA_HEADS = 8
DELTA_CONV = 4
DELTA_CHUNK = 64
DELTA_WIDTH = DELTA_HEADS * HEAD_DIM
CONV_HALO = 8
L2_EPS = 1e-6
GDN_CHUNKS_PER_STEP = 2


def _softplus(x):
    return jnp.maximum(x, 0.0) + jnp.log1p(jnp.exp(-jnp.abs(x)))


def _sigmoid(x):
    return 1.0 / (1.0 + jnp.exp(-x))


def _cd_small_kernel(x_ref, gk_ref, alog_ref, dt_ref, kidx_ref, w_ref, beta_ref, g_ref):
    x = x_ref[...]
    kidx_ref[...] = _rms_norm_rows(x[:, :IDX_DIM], gk_ref[...]).astype(kidx_ref.dtype)
    o = IDX_DIM
    w_ref[...] = x[:, o:o + IDX_HEADS]
    o += IDX_HEADS
    beta_ref[...] = _sigmoid(x[:, o:o + DELTA_HEADS])
    o += DELTA_HEADS
    g_ref[...] = -jnp.exp(alog_ref[...]) * _softplus(x[:, o:o + DELTA_HEADS] + dt_ref[...])


def _cd_small(small, norm_kidx, a_log, dt_bias, tm=1024):
    t, c = small.shape
    row = lambda n: pl.BlockSpec((1, n), lambda i: (0, 0))
    out = lambda n: pl.BlockSpec((tm, n), lambda i: (i, 0))
    return pl.pallas_call(
        _cd_small_kernel,
        grid=(t // tm,),
        in_specs=[pl.BlockSpec((tm, c), lambda i: (i, 0)), row(IDX_DIM), row(DELTA_HEADS), row(DELTA_HEADS)],
        out_specs=[out(IDX_DIM), out(IDX_HEADS), out(DELTA_HEADS), out(DELTA_HEADS)],
        out_shape=[jax.ShapeDtypeStruct((t, IDX_DIM), BF16), jax.ShapeDtypeStruct((t, IDX_HEADS), F32),
                   jax.ShapeDtypeStruct((t, DELTA_HEADS), F32), jax.ShapeDtypeStruct((t, DELTA_HEADS), F32)],
        compiler_params=_cparams("parallel"),
        name="cd_small_prep",
    )(small, norm_kidx.reshape(1, -1).astype(F32), a_log.reshape(1, -1).astype(F32),
      dt_bias.reshape(1, -1).astype(F32))


def _gdn_conv_kernel(x_ref, halo_ref, w_ref, o_ref, *, tiles_per_seq):
    tm = x_ref.shape[0]
    first = pl.program_id(0) % tiles_per_seq == 0
    ext = jnp.concatenate([jnp.where(first, 0.0, halo_ref[...]), x_ref[...]], axis=0)
    w = w_ref[...]
    y = ext * w[DELTA_CONV - 1:DELTA_CONV, :]
    for back in range(1, DELTA_CONV):
        y = y + pltpu.roll(ext, back, 0) * w[DELTA_CONV - 1 - back:DELTA_CONV - back, :]
    y = y[CONV_HALO:, :]
    y = y * _sigmoid(y)
    for hd in range(3 * DELTA_HEADS):
        cols = slice(hd * HEAD_DIM, (hd + 1) * HEAD_DIM)
        t = y[:, cols]
        if hd < 2 * DELTA_HEADS:
            t = t * lax.rsqrt(jnp.sum(t * t, axis=-1, keepdims=True) + L2_EPS)
        if hd < DELTA_HEADS:
            t = t * (HEAD_DIM ** -0.5)
        o_ref[:, cols] = t


def _gdn_conv(x, conv_w, seq, tm=256):
    t = x.shape[0]
    c = conv_w.shape[1]
    halo_blocks_per_tile = tm // CONV_HALO
    return pl.pallas_call(
        functools.partial(_gdn_conv_kernel, tiles_per_seq=seq // tm),
        grid=(t // tm,),
        in_specs=[
            pl.BlockSpec((tm, c), lambda i: (i, 0)),
            pl.BlockSpec((CONV_HALO, c), lambda i: (jnp.maximum(i * halo_blocks_per_tile - 1, 0), 0)),
            pl.BlockSpec((DELTA_CONV, c), lambda i: (0, 0)),
        ],
        out_specs=pl.BlockSpec((tm, c), lambda i: (i, 0)),
        out_shape=jax.ShapeDtypeStruct((t, c), F32),
        compiler_params=_cparams("parallel"),
        name="gdn_conv",
    )(x, x, conv_w.astype(F32))


def _exact_nt(a, b):
    return lax.dot_general(a, b, (((1,), (1,)), ((), ())), preferred_element_type=F32,
                           precision=lax.Precision.HIGHEST)


def _bdot(a, b):
    return jnp.dot(a.astype(BF16), b.astype(BF16), preferred_element_type=F32)


def _bdot_nt(a, b):
    return _nt_dot(a.astype(BF16), b.astype(BF16))


def _bdot_tn(a, b):
    return lax.dot_general(a.astype(BF16), b.astype(BF16), (((0,), (0,)), ((), ())), preferred_element_type=F32)


def _unit_lower_inverses(mats, eye):
    invs = [eye - a for a in mats]
    powers = list(mats)
    span = 2
    while span < mats[0].shape[0]:
        powers = [_bdot(p, p) for p in powers]
        invs = [inv + _bdot(inv, p) for inv, p in zip(invs, powers)]
        span *= 2
    return invs


def _gdn_kernel(q_ref, k_ref, v_ref, z_ref, g_ref, beta_ref, gain_ref, o_ref, state_ref):
    c = DELTA_CHUNK

    @pl.when(pl.program_id(1) == 0)
    def _():
        state_ref[...] = jnp.zeros_like(state_ref)

    row = lax.broadcasted_iota(jnp.int32, (c, c), 0)
    col = lax.broadcasted_iota(jnp.int32, (c, c), 1)
    lower = col <= row
    lower_f = jnp.where(lower, 1.0, 0.0)
    eye = jnp.where(col == row, 1.0, 0.0)
    gain = gain_ref[...]
    heads = range(DELTA_HEADS)
    units = [(n, hd) for n in range(GDN_CHUNKS_PER_STEP) for hd in heads]
    block = lambda ref, n, hd: ref[n * c:(n + 1) * c, hd * HEAD_DIM:(hd + 1) * HEAD_DIM]

    gc_rows = [_exact_nt(g_ref[n], lower_f) for n in range(GDN_CHUNKS_PER_STEP)]
    gc_cols = [_exact_nt(lower_f, g_ref[n]) for n in range(GDN_CHUNKS_PER_STEP)]
    beta_cols = [_exact_nt(eye, beta_ref[n]) for n in range(GDN_CHUNKS_PER_STEP)]
    gc_col = {(n, hd): gc_cols[n][:, hd:hd + 1] for n, hd in units}
    beta_col = {(n, hd): beta_cols[n][:, hd:hd + 1] for n, hd in units}
    decay = {u: jnp.exp(jnp.where(lower, gc_col[u] - gc_rows[u[0]][u[1]:u[1] + 1, :], NEG_INF)) for u in units}
    kb = {u: block(k_ref, *u) * beta_col[u] for u in units}
    kk = {u: _bdot_nt(kb[u], block(k_ref, *u)) for u in units}
    qk = {u: _bdot_nt(block(q_ref, *u), block(k_ref, *u)) for u in units}
    invs = _unit_lower_inverses([jnp.where(col < row, kk[u] * decay[u], 0.0) for u in units], eye)
    sol = {u: _bdot(inv, jnp.concatenate([block(v_ref, *u) * beta_col[u], kb[u] * jnp.exp(gc_col[u])], axis=1))
           for u, inv in zip(units, invs)}

    states = [state_ref[hd] for hd in heads]
    for n in range(GDN_CHUNKS_PER_STEP):
        g_last = {hd: gc_col[(n, hd)][c - 1:c, :] for hd in heads}
        both = [_bdot(jnp.concatenate([sol[(n, hd)][:, HEAD_DIM:], block(q_ref, n, hd) * jnp.exp(gc_col[(n, hd)])],
                                      axis=0), states[hd]) for hd in heads]
        v_new = [sol[(n, hd)][:, :HEAD_DIM] - both[hd][:c] for hd in heads]
        intra = [_bdot(qk[(n, hd)] * decay[(n, hd)], v_new[hd]) for hd in heads]
        grow = [_bdot_tn(block(k_ref, n, hd) * jnp.exp(g_last[hd] - gc_col[(n, hd)]), v_new[hd]) for hd in heads]
        for hd in heads:
            states[hd] = states[hd] * jnp.exp(g_last[hd]) + grow[hd]
            o = both[hd][c:] + intra[hd]
            z = block(z_ref, n, hd)
            normed = o * lax.rsqrt(jnp.mean(o * o, axis=-1, keepdims=True) + RMS_EPS) * gain
            o_ref[n * c:(n + 1) * c, hd * HEAD_DIM:(hd + 1) * HEAD_DIM] = (normed * (z * _sigmoid(z))).astype(o_ref.dtype)
    for hd in heads:
        state_ref[hd] = states[hd]


def _gated_delta(qkv, z_src, z_block, g, beta, norm_out, batch, seq):
    c = DELTA_CHUNK
    tt = GDN_CHUNKS_PER_STEP * c
    steps = seq // tt
    t = batch * seq
    by_chunk = lambda a: a.reshape(t // c, c, DELTA_HEADS).transpose(0, 2, 1)
    tok = lambda src_block: pl.BlockSpec((tt, DELTA_WIDTH), lambda b, i: (b * steps + i, src_block))
    chunk_rows = pl.BlockSpec((GDN_CHUNKS_PER_STEP, DELTA_HEADS, c), lambda b, i: (b * steps + i, 0, 0))
    return pl.pallas_call(
        _gdn_kernel,
        grid=(batch, steps),
        in_specs=[tok(0), tok(1), tok(2), tok(z_block), chunk_rows, chunk_rows,
                  pl.BlockSpec((1, HEAD_DIM), lambda b, i: (0, 0))],
        out_specs=pl.BlockSpec((tt, DELTA_WIDTH), lambda b, i: (b * steps + i, 0)),
        out_shape=jax.ShapeDtypeStruct((t, DELTA_WIDTH), BF16),
        scratch_shapes=[pltpu.VMEM((DELTA_HEADS, HEAD_DIM, HEAD_DIM), F32)],
        compiler_params=_cparams("parallel", "arbitrary"),
        name="gated_delta",
    )(qkv, qkv, qkv, z_src, by_chunk(g), by_chunk(beta), norm_out.reshape(1, HEAD_DIM).astype(F32))


SB_WIDTH = 1024
DSA_Q_RANK = 256
DSA_WIDTH = DSA_HEADS * HEAD_DIM


def _stick_pool_mixer(h, norm_g, w_in, pool_w, pool_scale, w_out, batch, seq):
    qkv = _linear([h], [w_in[:, :3 * SB_WIDTH]], norm_g=norm_g, out_dtype=BF16, tn=1024, name="in_ab_qkv")
    u = _linear([h], [w_in[:, 3 * SB_WIDTH:]], norm_g=norm_g, name="in_ab_pool")
    o_a = _sb_attention(qkv, batch, seq, SB_WIDTH // HEAD_DIM)
    o_b = _multiscale_pool(u, pool_w, pool_scale, seq)
    return _linear([o_a, o_b], [w_out[:SB_WIDTH], w_out[SB_WIDTH:]], residual=h, tn=1024, name="out_ab")


def _dsa_delta_mixer(h, norm_g, w_in, w_uq, w_iq, norm_cq, norm_kidx, conv_w, a_log, dt_bias, norm_out, w_out,
                     rel_bias, batch, seq):
    sizes = [DSA_Q_RANK, DSA_WIDTH, DSA_WIDTH, IDX_DIM, IDX_HEADS, 3 * DELTA_WIDTH, DELTA_HEADS, DELTA_HEADS,
             DELTA_WIDTH]
    offs = [0]
    for n in sizes:
        offs.append(offs[-1] + n)
    col = lambda a, b_: w_in[:, offs[a]:offs[b_]]
    pad = jnp.zeros((w_in.shape[0], 128 - (IDX_DIM + IDX_HEADS + 2 * DELTA_HEADS)), w_in.dtype)
    w_f32 = jnp.concatenate([col(5, 6), col(8, 9), col(0, 1), col(3, 5), col(6, 8), pad], axis=1)
    kv = _linear([h], [col(1, 3)], norm_g=norm_g, out_dtype=BF16, tn=1024, name="in_cd_kv")
    cd = _linear([h], [w_f32], norm_g=norm_g, tn=896, name="in_cd_rest")
    z_block = 3
    c_q = cd[:, 4 * DELTA_WIDTH:4 * DELTA_WIDTH + DSA_Q_RANK]
    small = cd[:, 4 * DELTA_WIDTH + DSA_Q_RANK:]
    qq = _linear([c_q], [jnp.concatenate([w_uq, w_iq], axis=1)], norm_g=norm_cq, out_dtype=BF16, name="dsa_queries")
    kidx, w_idx, beta, g = _cd_small(small, norm_kidx, a_log, dt_bias)
    mask = _dsa_select(qq, w_idx.T, kidx, batch, seq)
    o_c = _dsa_attention(qq, kv, mask, rel_bias, batch, seq)
    conv = _gdn_conv(cd, conv_w, seq)
    o_d = _gated_delta(conv, cd, z_block, g, beta, norm_out, batch, seq)
    return _linear([o_c, o_d], [w_out[:DSA_WIDTH], w_out[DSA_WIDTH:]], residual=h, tn=1024, name="out_cd")


def kernel(x, mem, norm_mix, norm_cross, norm_mem, norm_ffn, norm_final, w_in_ab, pool_w, pool_scale, w_out_ab, w_in_cd, w_uq, w_iq, norm_cq, norm_kidx, conv_w, a_log, dt_bias, norm_delta_out, w_out_cd, rel_bias, xattn_wq, xattn_wkv, xattn_wo, peer_wq, peer_subkeys, peer_u, peer_v):
    b, s, d = x.shape
    n_mem = mem.shape[1]
    depth = norm_mix.shape[0]
    h = x.reshape(b * s, d)
    mem2 = mem.reshape(b * n_mem, d)
    for layer in range(depth):
        j = layer // 2
        if layer % 2 == 0:
            h = _stick_pool_mixer(h, norm_mix[layer], w_in_ab[j], pool_w[j], pool_scale[j], w_out_ab[j], b, s)
        else:
            h = _dsa_delta_mixer(h, norm_mix[layer], w_in_cd[j], w_uq[j], w_iq[j], norm_cq[j], norm_kidx[j],
                                 conv_w[j], a_log[j], dt_bias[j], norm_delta_out[j], w_out_cd[j], rel_bias, b, s)
        kv = _linear([mem2], [xattn_wkv[layer]], norm_g=norm_mem[layer], out_dtype=BF16, tm=n_mem, name="xattn_kv")
        h = _cross_attention(h, norm_cross[layer], xattn_wq[layer], kv, xattn_wo[layer], s, n_mem)
        h = _peer_ffn(h, norm_ffn[layer], peer_wq[layer], peer_subkeys[layer], peer_u[layer], peer_v[layer])
    return _rmsnorm(h, norm_final).reshape(b, s, d)
```

```python
import functools
import math

import jax
import jax.numpy as jnp
from jax import lax
from jax.experimental import pallas as pl
from jax.experimental.pallas import tpu as pltpu

F32 = jnp.float32
BF16 = jnp.bfloat16

HEAD_DIM = 128
RMS_EPS = 1e-6
EXP_UNDERFLOW = -104.0
VMEM_LIMIT_BYTES = 56 * 1024 * 1024


def _cparams(*semantics):
    return pltpu.CompilerParams(dimension_semantics=semantics, vmem_limit_bytes=VMEM_LIMIT_BYTES)


def _nt_dot(a, b):
    return lax.dot_general(a, b, (((1,), (1,)), ((), ())), preferred_element_type=F32)


def _dot(a, b):
    return jnp.dot(a, b, preferred_element_type=F32)


def _rms_norm_rows(xf, g):
    return xf * lax.rsqrt(jnp.mean(xf * xf, axis=-1, keepdims=True) + RMS_EPS) * g


def _linear_kernel(*refs, n_lhs, has_norm, has_res):
    pos = 0
    x_refs = refs[pos:pos + n_lhs]; pos += n_lhs
    w_refs = refs[pos:pos + n_lhs]; pos += n_lhs
    g_ref = None
    if has_norm:
        g_ref = refs[pos]; pos += 1
    r_ref = None
    if has_res:
        r_ref = refs[pos]; pos += 1
    o_ref = refs[pos]; pos += 1
    xn_ref = refs[pos] if has_norm else None

    if has_norm:
        @pl.when(pl.program_id(1) == 0)
        def _():
            xn_ref[...] = _rms_norm_rows(x_refs[0][...].astype(F32), g_ref[...]).astype(BF16)
        acc = _dot(xn_ref[...], w_refs[0][...])
    else:
        acc = _dot(x_refs[0][...].astype(BF16), w_refs[0][...])
        for x_ref, w_ref in zip(x_refs[1:], w_refs[1:]):
            acc = acc + _dot(x_ref[...].astype(BF16), w_ref[...])
    if has_res:
        acc = acc + r_ref[...]
    o_ref[...] = acc.astype(o_ref.dtype)


def _linear(xs, ws, *, norm_g=None, residual=None, out_dtype=F32, tm=512, tn=None, name="linear"):
    xs = list(xs)
    ws = [w.astype(BF16) for w in ws]
    t = xs[0].shape[0]
    n = ws[0].shape[1]
    if tn is None:
        tn = n
    assert t % tm == 0 and n % tn == 0
    has_norm = norm_g is not None
    has_res = residual is not None
    assert not has_norm or len(xs) == 1
    in_specs = [pl.BlockSpec((tm, x.shape[1]), lambda i, j: (i, 0)) for x in xs]
    in_specs += [pl.BlockSpec((w.shape[0], tn), lambda i, j: (0, j)) for w in ws]
    args = xs + ws
    if has_norm:
        in_specs.append(pl.BlockSpec((1, xs[0].shape[1]), lambda i, j: (0, 0)))
        args.append(norm_g.reshape(1, -1).astype(F32))
    if has_res:
        in_specs.append(pl.BlockSpec((tm, tn), lambda i, j: (i, j)))
        args.append(residual)
    scratch = [pltpu.VMEM((tm, xs[0].shape[1]), BF16)] if has_norm else []
    return pl.pallas_call(
        functools.partial(_linear_kernel, n_lhs=len(xs), has_norm=has_norm, has_res=has_res),
        grid=(t // tm, n // tn),
        in_specs=in_specs,
        out_specs=pl.BlockSpec((tm, tn), lambda i, j: (i, j)),
        out_shape=jax.ShapeDtypeStruct((t, n), out_dtype),
        scratch_shapes=scratch,
        compiler_params=_cparams("parallel", "arbitrary"),
        name=name,
    )(*args)


SB_BLOCK = 256


SB_HEADS_PER_STEP = 2


def _sb_attn_kernel(q_ref, k_ref, v_ref, o_ref):
    tq = q_ref.shape[0]
    qi = pl.program_id(2)
    scale = HEAD_DIM ** -0.5
    heads = range(SB_HEADS_PER_STEP)
    head_cols = [slice(hd * HEAD_DIM, (hd + 1) * HEAD_DIM) for hd in heads]
    row = lax.broadcasted_iota(jnp.int32, (tq, tq), 0)
    col = lax.broadcasted_iota(jnp.int32, (tq, tq), 1)
    suffix = jnp.where(row > col, 1.0, 0.0).astype(BF16)
    before = col < row

    def block(j, carries, accs, diagonal):
        start = pl.multiple_of(j * tq, tq)
        z = [_nt_dot(q_ref[:, cols], k_ref[pl.ds(start, tq), cols]) * scale for cols in head_cols]
        lk = [-(jnp.maximum(zh, 0.0) + jnp.log1p(jnp.exp(-jnp.abs(zh)))) for zh in z]
        if diagonal:
            lk = [jnp.where(before, t, 0.0) for t in lk]
        lk_hi = [t.astype(BF16) for t in lk]
        lk_lo = [(t - hi.astype(F32)).astype(BF16) for t, hi in zip(lk, lk_hi)]
        right = [_dot(hi, suffix) + _dot(lo, suffix) for hi, lo in zip(lk_hi, lk_lo)]
        w = [jnp.exp(z[hd] + lk[hd] + right[hd] + carries[hd]) for hd in heads]
        if diagonal:
            w = [jnp.where(before, t, 0.0) for t in w]
        accs = tuple(accs[hd] + _dot(w[hd].astype(BF16), v_ref[pl.ds(start, tq), head_cols[hd]]) for hd in heads)
        carries = tuple(carries[hd] + right[hd][:, 0:1] + lk[hd][:, 0:1] for hd in heads)
        return carries, accs

    zeros = lambda width: tuple(jnp.zeros((tq, width), F32) for _ in heads)
    carries, accs = block(qi, zeros(1), zeros(HEAD_DIM), True)

    def cond(state):
        j, carries, _ = state
        live = carries[0]
        for c in carries[1:]:
            live = jnp.maximum(live, c)
        return jnp.logical_and(j >= 0, jnp.max(live) > EXP_UNDERFLOW)

    def body(state):
        j, carries, accs = state
        carries, accs = block(j, carries, accs, False)
        return j - 1, carries, accs

    _, _, accs = lax.while_loop(cond, body, (qi - 1, carries, accs))
    for hd in heads:
        o_ref[:, head_cols[hd]] = accs[hd].astype(o_ref.dtype)


def _sb_attention(qkv, batch, seq, heads):
    tq = SB_BLOCK
    nq = seq // tq
    groups = heads // SB_HEADS_PER_STEP
    width = SB_HEADS_PER_STEP * HEAD_DIM
    resident = lambda index_map: pl.BlockSpec((seq, width), index_map, pipeline_mode=pl.Buffered(1))
    return pl.pallas_call(
        _sb_attn_kernel,
        grid=(batch, groups, nq),
        in_specs=[
            pl.BlockSpec((tq, width), lambda b, h, i: (b * nq + i, h)),
            resident(lambda b, h, i: (b, groups + h)),
            resident(lambda b, h, i: (b, 2 * groups + h)),
        ],
        out_specs=pl.BlockSpec((tq, width), lambda b, h, i: (b * nq + i, h)),
        out_shape=jax.ShapeDtypeStruct((batch * seq, heads * HEAD_DIM), BF16),
        compiler_params=_cparams("parallel", "parallel", "arbitrary"),
        name="sb_attention",
    )(qkv, qkv, qkv)


POOL_WINDOWS = (2, 4, 8, 16)
POOL_DIM = 256
POOL_HALO = 16


def _pool_kernel(u_ref, halo_ref, w_ref, scale_ref, o_ref, *, tiles_per_seq):
    tm = u_ref.shape[0]
    tile_in_seq = pl.program_id(0) % tiles_per_seq
    u = u_ref[...]
    halo = jnp.where(tile_in_seq == 0, 0.0, halo_ref[...])
    ext = jnp.concatenate([halo, u], axis=0)
    pos = tile_in_seq * tm + lax.broadcasted_iota(jnp.int32, (tm, 1), 0)
    for g, win in enumerate(POOL_WINDOWS):
        cols = slice(g * POOL_DIM, (g + 1) * POOL_DIM)
        a = ext[:, cols]
        k = 1
        while k < win:
            a = a + pltpu.roll(a, k, 0)
            k *= 2
        count = jnp.minimum(pos + 1, win).astype(F32)
        d = a[POOL_HALO:, :] / count - u[:, cols]
        y = _dot(d.astype(BF16), w_ref[g]) * scale_ref[:, cols]
        o_ref[:, cols] = y.astype(o_ref.dtype)


def _multiscale_pool(u, pool_w, pool_scale, seq, tm=512):
    t, c = u.shape
    assert seq % tm == 0 and tm % POOL_HALO == 0 and max(POOL_WINDOWS) <= POOL_HALO
    halo_blocks_per_tile = tm // POOL_HALO
    return pl.pallas_call(
        functools.partial(_pool_kernel, tiles_per_seq=seq // tm),
        grid=(t // tm,),
        in_specs=[
            pl.BlockSpec((tm, c), lambda i: (i, 0)),
            pl.BlockSpec((POOL_HALO, c), lambda i: (jnp.maximum(i * halo_blocks_per_tile - 1, 0), 0)),
            pl.BlockSpec(pool_w.shape, lambda i: (0, 0, 0)),
            pl.BlockSpec((1, c), lambda i: (0, 0)),
        ],
        out_specs=pl.BlockSpec((tm, c), lambda i: (i, 0)),
        out_shape=jax.ShapeDtypeStruct((t, c), BF16),
        compiler_params=_cparams("parallel"),
        name="multiscale_pool",
    )(u, u, pool_w.astype(BF16), pool_scale.reshape(1, c).astype(F32))


XATTN_HEADS = 4


def _xattn_kernel(h_ref, g_ref, wq_ref, k_ref, v_ref, wo_ref, o_ref):
    scale = HEAD_DIM ** -0.5
    h = h_ref[...]
    hn = _rms_norm_rows(h, g_ref[...]).astype(BF16)
    q = _dot(hn, wq_ref[...]).astype(BF16)
    outs = []
    for hd in range(XATTN_HEADS):
        cols = slice(hd * HEAD_DIM, (hd + 1) * HEAD_DIM)
        logits = _nt_dot(q[:, cols], k_ref[:, cols]) * scale
        logits = logits - jnp.max(logits, axis=-1, keepdims=True)
        e = jnp.exp(logits)
        p = e / jnp.sum(e, axis=-1, keepdims=True)
        outs.append(_dot(p.astype(BF16), v_ref[:, cols]))
    o = jnp.concatenate(outs, axis=-1).astype(BF16)
    o_ref[...] = h + _dot(o, wo_ref[...])


def _cross_attention(h, norm_g, wq, kv, wo, seq, mem_tokens, tm=512):
    t, d = h.shape
    hw = XATTN_HEADS * HEAD_DIM
    tiles_per_seq = seq // tm
    return pl.pallas_call(
        _xattn_kernel,
        grid=(t // tm,),
        in_specs=[
            pl.BlockSpec((tm, d), lambda i: (i, 0)),
            pl.BlockSpec((1, d), lambda i: (0, 0)),
            pl.BlockSpec((d, hw), lambda i: (0, 0)),
            pl.BlockSpec((mem_tokens, hw), lambda i: (i // tiles_per_seq, 0)),
            pl.BlockSpec((mem_tokens, hw), lambda i: (i // tiles_per_seq, 1)),
            pl.BlockSpec((hw, d), lambda i: (0, 0)),
        ],
        out_specs=pl.BlockSpec((tm, d), lambda i: (i, 0)),
        out_shape=jax.ShapeDtypeStruct((t, d), F32),
        compiler_params=_cparams("parallel"),
        name="cross_attention",
    )(h, norm_g.reshape(1, d).astype(F32), wq.astype(BF16), kv, kv, wo.astype(BF16))


def _rmsnorm_kernel(x_ref, g_ref, o_ref):
    o_ref[...] = _rms_norm_rows(x_ref[...], g_ref[...])


def _rmsnorm(x, g, tm=512):
    t, d = x.shape
    return pl.pallas_call(
        _rmsnorm_kernel,
        grid=(t // tm,),
        in_specs=[pl.BlockSpec((tm, d), lambda i: (i, 0)), pl.BlockSpec((1, d), lambda i: (0, 0))],
        out_specs=pl.BlockSpec((tm, d), lambda i: (i, 0)),
        out_shape=jax.ShapeDtypeStruct((t, d), F32),
        compiler_params=_cparams("parallel"),
        name="final_rmsnorm",
    )(x, g.reshape(1, d).astype(F32))


PEER_HEADS = 8
PEER_KEYS = 128
PEER_TOPK = 16
NEG_INF = float("-inf")


PEER_PAIR_BLOCKS = ((0, 16), (16, 8), (24, 8), (32, 8), (40, 8), (48, 8), (56, 8), (64, 8))
PEER_PAIR_TAIL = 72
PEER_PAIR_ROWS = PEER_PAIR_TAIL + 8


def _top16_rows(s, vals_ref, idx_ref=None):
    rows = lax.broadcasted_iota(jnp.int32, s.shape, 0)
    work = s
    for r in range(PEER_TOPK):
        m = jnp.max(work, axis=0, keepdims=True)
        first = jnp.min(jnp.where(work == m, rows, PEER_KEYS), axis=0, keepdims=True)
        vals_ref[r:r + 1, :] = m
        if idx_ref is not None:
            idx_ref[r:r + 1, :] = first
        work = jnp.where(rows == first, NEG_INF, work)
    return work == NEG_INF


def _peer_route_kernel(q_ref, keys_ref, thr_ref, c1_ref, s2_ref, e2_ref, a_ref, b_ref, cand_ref, idx_ref):
    rows = lax.broadcasted_iota(jnp.int32, (PEER_KEYS, q_ref.shape[0]), 0)
    for hd in range(PEER_HEADS):
        s1 = _nt_dot(keys_ref[2 * hd], q_ref[:, (2 * hd) * PEER_KEYS:(2 * hd + 1) * PEER_KEYS])
        s2 = _nt_dot(keys_ref[2 * hd + 1], q_ref[:, (2 * hd + 1) * PEER_KEYS:(2 * hd + 2) * PEER_KEYS])
        top1 = _top16_rows(s1, a_ref, idx_ref)
        top2 = _top16_rows(s2, b_ref)
        a = a_ref[...]
        b = b_ref[...]
        rank_in_block = lax.broadcasted_iota(jnp.int32, (8, a.shape[1]), 0)
        for i, (off, n_rows) in enumerate(PEER_PAIR_BLOCKS):
            pair = a_ref[i:i + 1, :] + b_ref[0:n_rows, :]
            feasible = PEER_TOPK // (i + 1)
            if feasible < n_rows:
                pair = jnp.where(rank_in_block < feasible, pair, NEG_INF)
            cand_ref[off:off + n_rows, :] = pair
        cand_ref[PEER_PAIR_TAIL:PEER_PAIR_TAIL + 8, :] = a_ref[8:16, :] + b_ref[0:1, :]
        work = cand_ref[...]
        best = a[0:1, :] + b[0:1, :]
        taken = jnp.zeros_like(best)
        tau = best
        z = jnp.zeros_like(best)
        for _ in range(PEER_TOPK):
            m = jnp.max(work, axis=0, keepdims=True)
            eq = work == m
            cnt = jnp.sum(jnp.where(eq, 1.0, 0.0), axis=0, keepdims=True)
            room = PEER_TOPK - taken
            use = jnp.clip(jnp.minimum(cnt, room), 0.0, None)
            z = z + use * jnp.exp(m - best)
            tau = jnp.where(room > 0.0, m, tau)
            taken = taken + cnt
            work = jnp.where(eq, NEG_INF, work)
        idx = idx_ref[...]
        thr = jnp.full(s1.shape, jnp.inf, F32)
        for r in range(PEER_TOPK):
            if r < len(PEER_PAIR_BLOCKS):
                off, n_rows = PEER_PAIR_BLOCKS[r]
                kept_partner = jnp.where(cand_ref[off:off + n_rows, :] >= tau, b_ref[0:n_rows, :], jnp.inf)
                thr_r = jnp.min(kept_partner, axis=0, keepdims=True)
            else:
                row = PEER_PAIR_TAIL + r - len(PEER_PAIR_BLOCKS)
                thr_r = jnp.where(cand_ref[row:row + 1, :] >= tau, b[0:1, :], jnp.inf)
            thr = jnp.where(rows == idx[r:r + 1, :], thr_r, thr)
        thr_ref[hd] = thr
        c1_ref[hd] = jnp.where(top1, jnp.exp(s1 - a[0:1, :]), 0.0) / z
        s2_ref[hd] = jnp.where(top2, s2, NEG_INF)
        e2_ref[hd] = jnp.exp(s2 - b[0:1, :])


def _peer_route(q, subkeys, tm=256):
    t = q.shape[0]
    keys = subkeys.reshape(PEER_HEADS * 2, PEER_KEYS, PEER_KEYS).astype(BF16)
    big = jax.ShapeDtypeStruct((PEER_HEADS, PEER_KEYS, t), F32)
    big_spec = pl.BlockSpec((PEER_HEADS, PEER_KEYS, tm), lambda i: (0, 0, i))
    return pl.pallas_call(
        _peer_route_kernel,
        grid=(t // tm,),
        in_specs=[
            pl.BlockSpec((tm, q.shape[1]), lambda i: (i, 0)),
            pl.BlockSpec(keys.shape, lambda i: (0, 0, 0)),
        ],
        out_specs=[big_spec, big_spec, big_spec, big_spec],
        out_shape=[big, big, big, big],
        scratch_shapes=[
            pltpu.VMEM((PEER_TOPK, tm), F32),
            pltpu.VMEM((PEER_TOPK, tm), F32),
            pltpu.VMEM((PEER_PAIR_ROWS, tm), F32),
            pltpu.VMEM((PEER_TOPK, tm), jnp.int32),
        ],
        compiler_params=_cparams("parallel"),
        name="peer_route",
    )(q, keys)


PEER_I1_PER_TILE = 8


def _peer_expert_kernel(h_ref, g_ref, u_ref, v_ref, thr_ref, c1_ref, s2_ref, e2_ref, o_ref,
                        xn_ref, acc_ref, act_even_ref, act_odd_ref, *, n_tiles):
    e = pl.program_id(1)
    act_refs = (act_even_ref, act_odd_ref)

    def up_project(slot):
        act_refs[slot][...] = _nt_dot(u_ref[...], xn_ref[...])

    def finish(slot):
        act = act_refs[slot][...]
        act = 0.5 * act * (1.0 + lax.erf(act * (2.0 ** -0.5)))
        gates = []
        for i1 in range(PEER_I1_PER_TILE):
            gate = None
            for hd in range(PEER_HEADS):
                kept = jnp.where(s2_ref[hd] >= thr_ref[hd, i1:i1 + 1, :], e2_ref[hd], 0.0)
                term = c1_ref[hd, i1:i1 + 1, :] * kept
                gate = term if gate is None else gate + term
            gates.append(gate)
        weighted = (jnp.concatenate(gates, axis=0) * act).astype(BF16)
        acc_ref[...] += lax.dot_general(weighted, v_ref[...], (((0,), (0,)), ((), ())),
                                        preferred_element_type=F32)

    @pl.when(e == 0)
    def _():
        xn_ref[...] = _rms_norm_rows(h_ref[...], g_ref[...]).astype(BF16)
        acc_ref[...] = jnp.zeros_like(acc_ref)
        up_project(0)

    middle = jnp.logical_and(e > 0, e < n_tiles)

    @pl.when(jnp.logical_and(middle, e % 2 == 1))
    def _():
        up_project(1)
        finish(0)

    @pl.when(jnp.logical_and(middle, e % 2 == 0))
    def _():
        up_project(0)
        finish(1)

    @pl.when(e == n_tiles)
    def _():
        finish((n_tiles - 1) % 2)
        o_ref[...] = h_ref[...] + acc_ref[...]


def _peer_experts(h, norm_g, u_tab, v_tab, route, tm=512):
    thr, c1, s2, e2 = route
    t, d = h.shape
    n_exp = u_tab.shape[0]
    te = PEER_I1_PER_TILE * PEER_KEYS
    n_tiles = n_exp // te
    prev = lambda e: jnp.maximum(e - 1, 0)
    head_rows = pl.BlockSpec((PEER_HEADS, PEER_I1_PER_TILE, tm), lambda i, e: (0, prev(e), i))
    head_full = pl.BlockSpec((PEER_HEADS, PEER_KEYS, tm), lambda i, e: (0, 0, i))
    return pl.pallas_call(
        functools.partial(_peer_expert_kernel, n_tiles=n_tiles),
        grid=(t // tm, n_tiles + 1),
        in_specs=[
            pl.BlockSpec((tm, d), lambda i, e: (i, 0)),
            pl.BlockSpec((1, d), lambda i, e: (0, 0)),
            pl.BlockSpec((te, d), lambda i, e: (jnp.minimum(e, n_tiles - 1), 0)),
            pl.BlockSpec((te, d), lambda i, e: (prev(e), 0)),
            head_rows, head_rows, head_full, head_full,
        ],
        out_specs=pl.BlockSpec((tm, d), lambda i, e: (i, 0)),
        out_shape=jax.ShapeDtypeStruct((t, d), F32),
        scratch_shapes=[pltpu.VMEM((tm, d), BF16), pltpu.VMEM((tm, d), F32),
                        pltpu.VMEM((te, tm), F32), pltpu.VMEM((te, tm), F32)],
        compiler_params=_cparams("parallel", "arbitrary"),
        name="peer_experts",
    )(h, norm_g.reshape(1, d).astype(F32), u_tab.astype(BF16), v_tab.astype(BF16), thr, c1, s2, e2)


def _peer_ffn(h, norm_g, w_q, subkeys, u_tab, v_tab):
    q = _linear([h], [w_q], norm_g=norm_g, out_dtype=BF16, tn=1024, name="peer_q")
    return _peer_experts(h, norm_g, u_tab, v_tab, _peer_route(q, subkeys))


DSA_HEADS = 8
IDX_HEADS = 16
IDX_DIM = 64
DSA_TOPK = 256
DSA_QUERIES = 256
DSA_SELECT_KEYS = 256
DSA_ATTN_KEYS = 1024
DSA_TILE = 128
DSA_HEADS_PER_STEP = 2
MASKED = -1e30
REL_BUCKETS = 32
REL_MAX_DIST = 2048
DSA_BIAS_TILES = REL_MAX_DIST // DSA_TILE + 2
INT32_MIN = -2 ** 31


def _dsa_select_kernel(qi_ref, w_ref, kidx_ref, mask_ref, key_ref, *, topk):
    ck = DSA_SELECT_KEYS
    sub = DSA_TILE
    tq = qi_ref.shape[0]
    n_total = mask_ref.shape[0] // ck
    qb = pl.program_id(1)
    n_chunks = (qb + 1) * (tq // ck)
    qi = qi_ref[...]
    w = w_ref[...] * (IDX_HEADS ** -0.5) * (IDX_DIM ** -0.5)
    qpos = qb * tq + lax.broadcasted_iota(jnp.int32, (1, tq), 1)
    krow_sub = lax.broadcasted_iota(jnp.int32, (sub, 1), 0)
    krow = lax.broadcasted_iota(jnp.int32, (ck, 1), 0)

    def score_chunk(c, carry):
        for part in range(ck // sub):
            start = pl.multiple_of(c * ck + part * sub, sub)
            kc = kidx_ref[pl.ds(start, sub), :]
            sc = jnp.zeros((sub, tq), F32)
            for hd in range(IDX_HEADS):
                rel = jnp.maximum(_nt_dot(kc, qi[:, hd * IDX_DIM:(hd + 1) * IDX_DIM]), 0.0)
                sc = sc + w[hd:hd + 1, :] * rel
            sc = jnp.where(start + krow_sub <= qpos, sc + 0.0, NEG_INF)
            bits = pltpu.bitcast(sc, jnp.int32)
            key_ref[pl.ds(start, sub), :] = jnp.where(bits < 0, bits ^ jnp.int32(0x7FFFFFFF), bits)
        return carry

    lax.fori_loop(0, n_chunks, score_chunk, 0)

    @pl.when(n_chunks % 2 == 1)
    def _():
        key_ref[pl.ds(pl.multiple_of(n_chunks * ck, ck), ck), :] = jnp.full((ck, tq), INT32_MIN, jnp.int32)

    def count_ge(cand):
        def chunk_pair(c, acc):
            start = pl.multiple_of(c * (2 * ck), 2 * ck)
            hit = jnp.where(key_ref[pl.ds(start, 2 * ck), :] >= cand, 1, 0)
            return acc + jnp.sum(hit.reshape(2 * ck // 8, 8, tq), axis=0)
        n_pairs = lax.shift_right_logical(n_chunks + 1, 1)
        acc = lax.fori_loop(0, n_pairs, chunk_pair, jnp.zeros((8, tq), jnp.int32))
        return jnp.sum(acc, axis=0, keepdims=True)

    def unresolved(state):
        bit, _, held = state
        return jnp.logical_and(bit < 32, jnp.max(held.astype(F32)) > topk)

    def refine(state):
        bit, kth, held = state
        cand = kth + lax.shift_left(jnp.int32(1), 31 - bit)
        cnt = count_ge(cand)
        ok = cnt >= topk
        return bit + 1, jnp.where(ok, cand, kth), jnp.where(ok, cnt, held)

    _, kth, _ = lax.while_loop(
        unresolved, refine,
        (jnp.int32(0), jnp.full((1, tq), INT32_MIN, jnp.int32), jnp.full((1, tq), n_chunks * ck, jnp.int32)))

    def write_mask(c, carry):
        start = pl.multiple_of(c * ck, ck)
        keep = jnp.logical_and(key_ref[pl.ds(start, ck), :] >= kth, start + krow <= qpos)
        mask_ref[pl.ds(start, ck), :] = jnp.where(keep, 0.0, MASKED).astype(mask_ref.dtype)
        return carry

    lax.fori_loop(0, n_chunks, write_mask, 0)

    def write_rest(c, carry):
        start = pl.multiple_of(c * ck, ck)
        mask_ref[pl.ds(start, ck), :] = jnp.full((ck, tq), MASKED, mask_ref.dtype)
        return carry

    lax.fori_loop(n_chunks, n_total, write_rest, 0)


def _dsa_select(qq, w_t, kidx, batch, seq):
    tq = DSA_QUERIES
    assert seq % (2 * DSA_SELECT_KEYS) == 0 and tq == DSA_SELECT_KEYS and DSA_SELECT_KEYS % DSA_TILE == 0
    nq = seq // tq
    topk = min(DSA_TOPK, seq // 4)
    iw = IDX_HEADS * IDX_DIM
    return pl.pallas_call(
        functools.partial(_dsa_select_kernel, topk=topk),
        grid=(batch, nq),
        in_specs=[
            pl.BlockSpec((tq, iw), lambda b, i: (b * nq + i, 1)),
            pl.BlockSpec((IDX_HEADS, tq), lambda b, i: (0, b * nq + i)),
            pl.BlockSpec((seq, IDX_DIM), lambda b, i: (b, 0)),
        ],
        out_specs=pl.BlockSpec((None, seq, tq), lambda b, i: (b, 0, i)),
        out_shape=jax.ShapeDtypeStruct((batch, seq, seq), BF16),
        scratch_shapes=[pltpu.VMEM((seq, tq), jnp.int32)],
        compiler_params=_cparams("parallel", "arbitrary"),
        name="dsa_select",
    )(qq, w_t, kidx)


def _dsa_attn_kernel(q_ref, k_ref, v_ref, mask_ref, bias_ref, o_ref):
    ck = DSA_ATTN_KEYS
    tile = DSA_TILE
    tq = q_ref.shape[0]
    i = pl.program_id(2)
    scale = HEAD_DIM ** -0.5
    n_steps = lax.div((i + 1) * tq + (ck - 1), ck)

    def step(c, state):
        start = pl.multiple_of(c * ck, ck)
        mask = mask_ref[pl.ds(start, ck), :].astype(F32)
        base = i * (tq // tile) - c * (ck // tile)
        offsets = [[jnp.clip(base + b - a, 0, DSA_BIAS_TILES - 1) for b in range(tq // tile)]
                   for a in range(ck // tile)]
        head_cols = [slice(hd * HEAD_DIM, (hd + 1) * HEAD_DIM) for hd in range(DSA_HEADS_PER_STEP)]
        dots = [_nt_dot(k_ref[pl.ds(start, ck), cols], q_ref[:, cols]) for cols in head_cols]
        new_state = []
        for hd, cols in enumerate(head_cols):
            m, l, acc = state[hd]
            bias = jnp.concatenate(
                [jnp.concatenate([bias_ref[hd, off] for off in row], axis=1) for row in offsets], axis=0)
            s = dots[hd] * scale + bias + mask
            m_new = jnp.maximum(m, jnp.max(s, axis=0, keepdims=True))
            alpha = jnp.exp(m - m_new)
            p = jnp.exp(s - m_new)
            l = l * alpha + jnp.sum(p, axis=0, keepdims=True)
            pv = lax.dot_general(v_ref[pl.ds(start, ck), cols], p.astype(BF16), (((0,), (0,)), ((), ())),
                                 preferred_element_type=F32)
            new_state.append((m_new, l, acc * alpha + pv))
        return tuple(new_state)

    init = (jnp.full((1, tq), MASKED, F32), jnp.zeros((1, tq), F32), jnp.zeros((HEAD_DIM, tq), F32))
    state = lax.fori_loop(0, n_steps, step, (init,) * DSA_HEADS_PER_STEP)
    for hd in range(DSA_HEADS_PER_STEP):
        _, l, acc = state[hd]
        o_ref[:, hd * HEAD_DIM:(hd + 1) * HEAD_DIM] = (acc / l).T.astype(o_ref.dtype)


def _t5_bucket(dist):
    n = jnp.maximum(dist, 0)
    exact = REL_BUCKETS // 2
    nf = jnp.maximum(n, 1).astype(F32)
    log_ratio = jnp.log(nf / exact) / math.log(REL_MAX_DIST / exact)
    large = exact + (log_ratio * (REL_BUCKETS - exact)).astype(jnp.int32)
    return jnp.where(n < exact, n, jnp.minimum(large, REL_BUCKETS - 1))


def _dsa_bias_tiles(rel_bias):
    tile = DSA_TILE
    heads = rel_bias.shape[1]
    offset = jnp.arange(DSA_BIAS_TILES - 1)[:, None, None] * tile
    bucket = _t5_bucket(offset + jnp.arange(tile)[None, None, :] - jnp.arange(tile)[None, :, None])
    near = jnp.zeros((heads,) + bucket.shape, F32)
    for bkt in range(REL_BUCKETS):
        near = jnp.where(bucket[None] == bkt, rel_bias[bkt].astype(F32)[:, None, None, None], near)
    far = jnp.broadcast_to(rel_bias[REL_BUCKETS - 1].astype(F32)[:, None, None, None], (heads, 1, tile, tile))
    return jnp.concatenate([near, far], axis=1)


def _dsa_attention(qq, kv, mask, rel_bias, batch, seq):
    tq = DSA_QUERIES
    assert seq % DSA_ATTN_KEYS == 0 and tq % DSA_TILE == 0 and DSA_ATTN_KEYS % DSA_TILE == 0
    nq = seq // tq
    groups = DSA_HEADS // DSA_HEADS_PER_STEP
    width = DSA_HEADS_PER_STEP * HEAD_DIM
    resident = lambda index_map: pl.BlockSpec((seq, width), index_map, pipeline_mode=pl.Buffered(1))
    return pl.pallas_call(
        _dsa_attn_kernel,
        grid=(batch, groups, nq),
        in_specs=[
            pl.BlockSpec((tq, width), lambda b, h, i: (b * nq + i, h)),
            resident(lambda b, h, i: (b, h)),
            resident(lambda b, h, i: (b, groups + h)),
            pl.BlockSpec((None, seq, tq), lambda b, h, i: (b, 0, i)),
            pl.BlockSpec((DSA_HEADS_PER_STEP, DSA_BIAS_TILES, DSA_TILE, DSA_TILE), lambda b, h, i: (h, 0, 0, 0)),
        ],
        out_specs=pl.BlockSpec((tq, width), lambda b, h, i: (b * nq + i, h)),
        out_shape=jax.ShapeDtypeStruct((batch * seq, DSA_HEADS * HEAD_DIM), BF16),
        compiler_params=_cparams("parallel", "parallel", "arbitrary"),
        name="dsa_attention",
    )(qq, kv, kv, mask, _dsa_bias_tiles(rel_bias))


DELTA_HEADS = 8
DELTA_CONV = 4
DELTA_CHUNK = 64
DELTA_WIDTH = DELTA_HEADS * HEAD_DIM
CONV_HALO = 8
L2_EPS = 1e-6
GDN_CHUNKS_PER_STEP = 2


def _softplus(x):
    return jnp.maximum(x, 0.0) + jnp.log1p(jnp.exp(-jnp.abs(x)))


def _sigmoid(x):
    return 1.0 / (1.0 + jnp.exp(-x))


def _cd_small_kernel(x_ref, gk_ref, alog_ref, dt_ref, kidx_ref, w_ref, beta_ref, g_ref):
    x = x_ref[...]
    kidx_ref[...] = _rms_norm_rows(x[:, :IDX_DIM], gk_ref[...]).astype(kidx_ref.dtype)
    o = IDX_DIM
    w_ref[...] = x[:, o:o + IDX_HEADS]
    o += IDX_HEADS
    beta_ref[...] = _sigmoid(x[:, o:o + DELTA_HEADS])
    o += DELTA_HEADS
    g_ref[...] = -jnp.exp(alog_ref[...]) * _softplus(x[:, o:o + DELTA_HEADS] + dt_ref[...])


def _cd_small(small, norm_kidx, a_log, dt_bias, tm=1024):
    t, c = small.shape
    row = lambda n: pl.BlockSpec((1, n), lambda i: (0, 0))
    out = lambda n: pl.BlockSpec((tm, n), lambda i: (i, 0))
    return pl.pallas_call(
        _cd_small_kernel,
        grid=(t // tm,),
        in_specs=[pl.BlockSpec((tm, c), lambda i: (i, 0)), row(IDX_DIM), row(DELTA_HEADS), row(DELTA_HEADS)],
        out_specs=[out(IDX_DIM), out(IDX_HEADS), out(DELTA_HEADS), out(DELTA_HEADS)],
        out_shape=[jax.ShapeDtypeStruct((t, IDX_DIM), BF16), jax.ShapeDtypeStruct((t, IDX_HEADS), F32),
                   jax.ShapeDtypeStruct((t, DELTA_HEADS), F32), jax.ShapeDtypeStruct((t, DELTA_HEADS), F32)],
        compiler_params=_cparams("parallel"),
        name="cd_small_prep",
    )(small, norm_kidx.reshape(1, -1).astype(F32), a_log.reshape(1, -1).astype(F32),
      dt_bias.reshape(1, -1).astype(F32))


def _gdn_conv_kernel(x_ref, halo_ref, w_ref, o_ref, *, tiles_per_seq):
    tm = x_ref.shape[0]
    first = pl.program_id(0) % tiles_per_seq == 0
    ext = jnp.concatenate([jnp.where(first, 0.0, halo_ref[...]), x_ref[...]], axis=0)
    w = w_ref[...]
    y = ext * w[DELTA_CONV - 1:DELTA_CONV, :]
    for back in range(1, DELTA_CONV):
        y = y + pltpu.roll(ext, back, 0) * w[DELTA_CONV - 1 - back:DELTA_CONV - back, :]
    y = y[CONV_HALO:, :]
    y = y * _sigmoid(y)
    for hd in range(3 * DELTA_HEADS):
        cols = slice(hd * HEAD_DIM, (hd + 1) * HEAD_DIM)
        t = y[:, cols]
        if hd < 2 * DELTA_HEADS:
            t = t * lax.rsqrt(jnp.sum(t * t, axis=-1, keepdims=True) + L2_EPS)
        if hd < DELTA_HEADS:
            t = t * (HEAD_DIM ** -0.5)
        o_ref[:, cols] = t


def _gdn_conv(x, conv_w, seq, tm=256):
    t = x.shape[0]
    c = conv_w.shape[1]
    halo_blocks_per_tile = tm // CONV_HALO
    return pl.pallas_call(
        functools.partial(_gdn_conv_kernel, tiles_per_seq=seq // tm),
        grid=(t // tm,),
        in_specs=[
            pl.BlockSpec((tm, c), lambda i: (i, 0)),
            pl.BlockSpec((CONV_HALO, c), lambda i: (jnp.maximum(i * halo_blocks_per_tile - 1, 0), 0)),
            pl.BlockSpec((DELTA_CONV, c), lambda i: (0, 0)),
        ],
        out_specs=pl.BlockSpec((tm, c), lambda i: (i, 0)),
        out_shape=jax.ShapeDtypeStruct((t, c), F32),
        compiler_params=_cparams("parallel"),
        name="gdn_conv",
    )(x, x, conv_w.astype(F32))


def _exact_nt(a, b):
    return lax.dot_general(a, b, (((1,), (1,)), ((), ())), preferred_element_type=F32,
                           precision=lax.Precision.HIGHEST)


def _bdot(a, b):
    return jnp.dot(a.astype(BF16), b.astype(BF16), preferred_element_type=F32)


def _bdot_nt(a, b):
    return _nt_dot(a.astype(BF16), b.astype(BF16))


def _bdot_tn(a, b):
    return lax.dot_general(a.astype(BF16), b.astype(BF16), (((0,), (0,)), ((), ())), preferred_element_type=F32)


def _unit_lower_inverses(mats, eye):
    invs = [eye - a for a in mats]
    powers = list(mats)
    span = 2
    while span < mats[0].shape[0]:
        powers = [_bdot(p, p) for p in powers]
        invs = [inv + _bdot(inv, p) for inv, p in zip(invs, powers)]
        span *= 2
    return invs


def _gdn_kernel(q_ref, k_ref, v_ref, z_ref, g_ref, beta_ref, gain_ref, o_ref, state_ref):
    c = DELTA_CHUNK

    @pl.when(pl.program_id(1) == 0)
    def _():
        state_ref[...] = jnp.zeros_like(state_ref)

    row = lax.broadcasted_iota(jnp.int32, (c, c), 0)
    col = lax.broadcasted_iota(jnp.int32, (c, c), 1)
    lower = col <= row
    lower_f = jnp.where(lower, 1.0, 0.0)
    eye = jnp.where(col == row, 1.0, 0.0)
    gain = gain_ref[...]
    heads = range(DELTA_HEADS)
    units = [(n, hd) for n in range(GDN_CHUNKS_PER_STEP) for hd in heads]
    block = lambda ref, n, hd: ref[n * c:(n + 1) * c, hd * HEAD_DIM:(hd + 1) * HEAD_DIM]

    gc_rows = [_exact_nt(g_ref[n], lower_f) for n in range(GDN_CHUNKS_PER_STEP)]
    gc_cols = [_exact_nt(lower_f, g_ref[n]) for n in range(GDN_CHUNKS_PER_STEP)]
    beta_cols = [_exact_nt(eye, beta_ref[n]) for n in range(GDN_CHUNKS_PER_STEP)]
    gc_col = {(n, hd): gc_cols[n][:, hd:hd + 1] for n, hd in units}
    beta_col = {(n, hd): beta_cols[n][:, hd:hd + 1] for n, hd in units}
    decay = {u: jnp.exp(jnp.where(lower, gc_col[u] - gc_rows[u[0]][u[1]:u[1] + 1, :], NEG_INF)) for u in units}
    kb = {u: block(k_ref, *u) * beta_col[u] for u in units}
    kk = {u: _bdot_nt(kb[u], block(k_ref, *u)) for u in units}
    qk = {u: _bdot_nt(block(q_ref, *u), block(k_ref, *u)) for u in units}
    invs = _unit_lower_inverses([jnp.where(col < row, kk[u] * decay[u], 0.0) for u in units], eye)
    sol = {u: _bdot(inv, jnp.concatenate([block(v_ref, *u) * beta_col[u], kb[u] * jnp.exp(gc_col[u])], axis=1))
           for u, inv in zip(units, invs)}

    states = [state_ref[hd] for hd in heads]
    for n in range(GDN_CHUNKS_PER_STEP):
        g_last = {hd: gc_col[(n, hd)][c - 1:c, :] for hd in heads}
        both = [_bdot(jnp.concatenate([sol[(n, hd)][:, HEAD_DIM:], block(q_ref, n, hd) * jnp.exp(gc_col[(n, hd)])],
                                      axis=0), states[hd]) for hd in heads]
        v_new = [sol[(n, hd)][:, :HEAD_DIM] - both[hd][:c] for hd in heads]
        intra = [_bdot(qk[(n, hd)] * decay[(n, hd)], v_new[hd]) for hd in heads]
        grow = [_bdot_tn(block(k_ref, n, hd) * jnp.exp(g_last[hd] - gc_col[(n, hd)]), v_new[hd]) for hd in heads]
        for hd in heads:
            states[hd] = states[hd] * jnp.exp(g_last[hd]) + grow[hd]
            o = both[hd][c:] + intra[hd]
            z = block(z_ref, n, hd)
            normed = o * lax.rsqrt(jnp.mean(o * o, axis=-1, keepdims=True) + RMS_EPS) * gain
            o_ref[n * c:(n + 1) * c, hd * HEAD_DIM:(hd + 1) * HEAD_DIM] = (normed * (z * _sigmoid(z))).astype(o_ref.dtype)
    for hd in heads:
        state_ref[hd] = states[hd]


def _gated_delta(qkv, z_src, z_block, g, beta, norm_out, batch, seq):
    c = DELTA_CHUNK
    tt = GDN_CHUNKS_PER_STEP * c
    steps = seq // tt
    t = batch * seq
    by_chunk = lambda a: a.reshape(t // c, c, DELTA_HEADS).transpose(0, 2, 1)
    tok = lambda src_block: pl.BlockSpec((tt, DELTA_WIDTH), lambda b, i: (b * steps + i, src_block))
    chunk_rows = pl.BlockSpec((GDN_CHUNKS_PER_STEP, DELTA_HEADS, c), lambda b, i: (b * steps + i, 0, 0))
    return pl.pallas_call(
        _gdn_kernel,
        grid=(batch, steps),
        in_specs=[tok(0), tok(1), tok(2), tok(z_block), chunk_rows, chunk_rows,
                  pl.BlockSpec((1, HEAD_DIM), lambda b, i: (0, 0))],
        out_specs=pl.BlockSpec((tt, DELTA_WIDTH), lambda b, i: (b * steps + i, 0)),
        out_shape=jax.ShapeDtypeStruct((t, DELTA_WIDTH), BF16),
        scratch_shapes=[pltpu.VMEM((DELTA_HEADS, HEAD_DIM, HEAD_DIM), F32)],
        compiler_params=_cparams("parallel", "arbitrary"),
        name="gated_delta",
    )(qkv, qkv, qkv, z_src, by_chunk(g), by_chunk(beta), norm_out.reshape(1, HEAD_DIM).astype(F32))


SB_WIDTH = 1024
DSA_Q_RANK = 256
DSA_WIDTH = DSA_HEADS * HEAD_DIM


def _stick_pool_mixer(h, norm_g, w_in, pool_w, pool_scale, w_out, batch, seq):
    qkv = _linear([h], [w_in[:, :3 * SB_WIDTH]], norm_g=norm_g, out_dtype=BF16, tn=1024, name="in_ab_qkv")
    u = _linear([h], [w_in[:, 3 * SB_WIDTH:]], norm_g=norm_g, name="in_ab_pool")
    o_a = _sb_attention(qkv, batch, seq, SB_WIDTH // HEAD_DIM)
    o_b = _multiscale_pool(u, pool_w, pool_scale, seq)
    return _linear([o_a, o_b], [w_out[:SB_WIDTH], w_out[SB_WIDTH:]], residual=h, tn=1024, name="out_ab")


def _dsa_delta_mixer(h, norm_g, w_in, w_uq, w_iq, norm_cq, norm_kidx, conv_w, a_log, dt_bias, norm_out, w_out,
                     rel_bias, batch, seq):
    sizes = [DSA_Q_RANK, DSA_WIDTH, DSA_WIDTH, IDX_DIM, IDX_HEADS, 3 * DELTA_WIDTH, DELTA_HEADS, DELTA_HEADS,
             DELTA_WIDTH]
    offs = [0]
    for n in sizes:
        offs.append(offs[-1] + n)
    col = lambda a, b_: w_in[:, offs[a]:offs[b_]]
    pad = jnp.zeros((w_in.shape[0], 128 - (IDX_DIM + IDX_HEADS + 2 * DELTA_HEADS)), w_in.dtype)
    w_f32 = jnp.concatenate([col(5, 6), col(8, 9), col(0, 1), col(3, 5), col(6, 8), pad], axis=1)
    kv = _linear([h], [col(1, 3)], norm_g=norm_g, out_dtype=BF16, tn=1024, name="in_cd_kv")
    cd = _linear([h], [w_f32], norm_g=norm_g, tn=896, name="in_cd_rest")
    z_block = 3
    c_q = cd[:, 4 * DELTA_WIDTH:4 * DELTA_WIDTH + DSA_Q_RANK]
    small = cd[:, 4 * DELTA_WIDTH + DSA_Q_RANK:]
    qq = _linear([c_q], [jnp.concatenate([w_uq, w_iq], axis=1)], norm_g=norm_cq, out_dtype=BF16, name="dsa_queries")
    kidx, w_idx, beta, g = _cd_small(small, norm_kidx, a_log, dt_bias)
    mask = _dsa_select(qq, w_idx.T, kidx, batch, seq)
    o_c = _dsa_attention(qq, kv, mask, rel_bias, batch, seq)
    conv = _gdn_conv(cd, conv_w, seq)
    o_d = _gated_delta(conv, cd, z_block, g, beta, norm_out, batch, seq)
    return _linear([o_c, o_d], [w_out[:DSA_WIDTH], w_out[DSA_WIDTH:]], residual=h, tn=1024, name="out_cd")


def kernel(x, mem, norm_mix, norm_cross, norm_mem, norm_ffn, norm_final, w_in_ab, pool_w, pool_scale, w_out_ab, w_in_cd, w_uq, w_iq, norm_cq, norm_kidx, conv_w, a_log, dt_bias, norm_delta_out, w_out_cd, rel_bias, xattn_wq, xattn_wkv, xattn_wo, peer_wq, peer_subkeys, peer_u, peer_v):
    b, s, d = x.shape
    n_mem = mem.shape[1]
    depth = norm_mix.shape[0]
    h = x.reshape(b * s, d)
    mem2 = mem.reshape(b * n_mem, d)
    for layer in range(depth):
        j = layer // 2
        if layer % 2 == 0:
            h = _stick_pool_mixer(h, norm_mix[layer], w_in_ab[j], pool_w[j], pool_scale[j], w_out_ab[j], b, s)
        else:
            h = _dsa_delta_mixer(h, norm_mix[layer], w_in_cd[j], w_uq[j], w_iq[j], norm_cq[j], norm_kidx[j],
                                 conv_w[j], a_log[j], dt_bias[j], norm_delta_out[j], w_out_cd[j], rel_bias, b, s)
        kv = _linear([mem2], [xattn_wkv[layer]], norm_g=norm_mem[layer], out_dtype=BF16, tm=n_mem, name="xattn_kv")
        h = _cross_attention(h, norm_cross[layer], xattn_wq[layer], kv, xattn_wo[layer], s, n_mem)
        h = _peer_ffn(h, norm_ffn[layer], peer_wq[layer], peer_subkeys[layer], peer_u[layer], peer_v[layer])
    return _rmsnorm(h, norm_final).reshape(b, s, d)
```
